```python
import jax, jax.numpy as jnp
from jax import lax
import numpy as np

D_MODEL = 1024
BATCH = 8
SEQ = 4096
DEPTH = 1

EPS = 1e-6
GMLP_WIDTH = 768
GMLP_GROUPS = 4
GMLP_GROUP_DIM = GMLP_WIDTH // GMLP_GROUPS
GMLP_CHUNK = 128
HEAD_DIM = 64
HEADS_PER_GROUP = 4
DILATED_GROUPS = ((128, 1), (512, 4), (2048, 16))
N_ATTN_GROUPS = 3
N_ATTN_HEADS = N_ATTN_GROUPS * HEADS_PER_GROUP
ATTN_WIDTH = N_ATTN_HEADS * HEAD_DIM
ATTN_OUT_WIDTH = HEADS_PER_GROUP * HEAD_DIM
ATTN_BLOCK = 128
ROPE_THETA = 500000.0
ROT_DIM = HEAD_DIM // 4
N_BRANCHES = 2
IN_WIDTH = 2 * GMLP_WIDTH + 3 * ATTN_WIDTH + N_BRANCHES * D_MODEL
D_FF = 2816
CONV_WIDTH = 3

kernel_name = "hybrid_gmlp_dilated_attn_convffn"


def rms_norm(x, g):
    xf = x.astype(jnp.float32)
    y = xf * lax.rsqrt(jnp.mean(xf * xf, axis=-1, keepdims=True) + EPS)
    return (y * g.astype(jnp.float32)).astype(x.dtype)


def apply_partial_rope(t, cos, sin):
    half = ROT_DIM // 2
    rot = t[..., :ROT_DIM].astype(jnp.float32)
    t1, t2 = rot[..., :half], rot[..., half:]
    rotated = jnp.concatenate([t1 * cos - t2 * sin, t1 * sin + t2 * cos], axis=-1)
    return jnp.concatenate([rotated.astype(t.dtype), t[..., ROT_DIM:]], axis=-1)


def chunked_spatial_gating(z, norm_g, w_s, b_s):
    b, s, _ = z.shape
    u, v = z[..., :GMLP_WIDTH], z[..., GMLP_WIDTH:]
    v = rms_norm(v, norm_g)
    n_chunks = s // GMLP_CHUNK
    vg = v.reshape(b, n_chunks, GMLP_CHUNK, GMLP_GROUPS, GMLP_GROUP_DIM)
    causal = jnp.tril(jnp.ones((GMLP_CHUNK, GMLP_CHUNK), dtype=bool))
    w = jnp.where(causal[None], w_s, jnp.zeros_like(w_s))
    mixed = jnp.einsum('gts,bcsgd->bctgd', w, vg) + b_s.T[None, None, :, :, None]
    return u * mixed.reshape(b, s, GMLP_WIDTH)


def banded_window_attention(q, k, v, steps):
    n, l, h, dh = q.shape
    nb = -(-l // ATTN_BLOCK)
    pad = nb * ATTN_BLOCK - l

    def blocks(t):
        t = jnp.pad(t, ((0, 0), (0, pad), (0, 0), (0, 0)))
        return t.reshape(n, nb, ATTN_BLOCK, h, dh)

    def with_prev(t):
        prev = jnp.pad(t, ((0, 0), (1, 0), (0, 0), (0, 0), (0, 0)))[:, :-1]
        return jnp.concatenate([prev, t], axis=2)

    qb = blocks(q)
    kc = with_prev(blocks(k))
    vc = with_prev(blocks(v))
    scores = jnp.einsum('nbqhd,nbkhd->nbhqk', qb, kc,
                        preferred_element_type=jnp.float32) * (HEAD_DIM ** -0.5)
    qi = jnp.arange(ATTN_BLOCK)[:, None]
    ki = jnp.arange(2 * ATTN_BLOCK)[None, :]
    dist = ATTN_BLOCK + qi - ki
    band = (dist >= 0) & (dist <= steps)
    first_ok = (jnp.arange(nb)[:, None, None] > 0) | (ki[None] >= ATTN_BLOCK)
    valid = band[None] & first_ok
    scores = jnp.where(valid[None, :, None], scores, -jnp.inf)
    m = jnp.max(scores, axis=-1, keepdims=True)
    p = jnp.exp(scores - m)
    den = jnp.sum(p, axis=-1, keepdims=True)
    out = jnp.einsum('nbhqk,nbkhd->nbqhd', (p / den).astype(v.dtype), vc)
    lse = (m + jnp.log(den))[..., 0]
    lse = lse.transpose(0, 1, 3, 2).reshape(n, nb * ATTN_BLOCK, h)[:, :l]
    out = out.reshape(n, nb * ATTN_BLOCK, h, dh)[:, :l]
    return out, lse


def dilated_mixture_attention(q, k, v):
    b, s = q.shape[:2]
    outs, lses = [], []
    for g, (window, dil) in enumerate(DILATED_GROUPS):
        lo, hi = g * HEADS_PER_GROUP, (g + 1) * HEADS_PER_GROUP
        sub = s // dil

        def to_residue(t):
            t = t[:, :, lo:hi].reshape(b, sub, dil, HEADS_PER_GROUP, HEAD_DIM)
            return t.transpose(0, 2, 1, 3, 4).reshape(b * dil, sub, HEADS_PER_GROUP, HEAD_DIM)

        o, lse = banded_window_attention(to_residue(q), to_residue(k), to_residue(v), window // dil)
        o = o.reshape(b, dil, sub, HEADS_PER_GROUP, HEAD_DIM).transpose(0, 2, 1, 3, 4)
        lse = lse.reshape(b, dil, sub, HEADS_PER_GROUP).transpose(0, 2, 1, 3)
        outs.append(o.reshape(b, s, HEADS_PER_GROUP, HEAD_DIM))
        lses.append(lse.reshape(b, s, HEADS_PER_GROUP))
    o_all = jnp.stack(outs, axis=3)
    alpha = jax.nn.softmax(jnp.stack(lses, axis=-1), axis=-1)
    mixed = jnp.einsum('bshg,bshgd->bshd', alpha.astype(o_all.dtype), o_all)
    return mixed.reshape(b, s, ATTN_OUT_WIDTH)


def causal_depthwise_conv(a, w, bias):
    s = a.shape[1]
    a_pad = jnp.pad(a, ((0, 0), (CONV_WIDTH - 1, 0), (0, 0)))
    y = a_pad[:, 0:s] * w[0]
    for j in range(1, CONV_WIDTH):
        y = y + a_pad[:, j:j + s] * w[j]
    return y + bias


def setup_inputs(seed: int = 0) -> dict:
    key = jax.random.key(seed)
    ks = jax.random.split(key, 16)
    f32 = jnp.float32
    nrm = lambda k, shape, scale: jax.random.normal(k, shape, f32) * scale
    return {
        "x": nrm(ks[0], (BATCH, SEQ, D_MODEL), 1.0),
        "positions": jnp.broadcast_to(jnp.arange(SEQ, dtype=jnp.int32), (BATCH, SEQ)),
        "mix_norm_g": 1.0 + nrm(ks[1], (DEPTH, D_MODEL), 0.02),
        "w_in": nrm(ks[2], (DEPTH, D_MODEL, IN_WIDTH), D_MODEL ** -0.5),
        "gmlp_norm_g": 1.0 + nrm(ks[3], (DEPTH, GMLP_WIDTH), 0.02),
        "w_spatial": nrm(ks[4], (DEPTH, GMLP_GROUPS, GMLP_CHUNK, GMLP_CHUNK), GMLP_CHUNK ** -0.5),
        "b_spatial": 1.0 + nrm(ks[5], (DEPTH, GMLP_GROUPS, GMLP_CHUNK), 0.1),
        "w_branch_a": nrm(ks[6], (DEPTH, GMLP_WIDTH, D_MODEL), GMLP_WIDTH ** -0.5),
        "w_branch_b": nrm(ks[7], (DEPTH, ATTN_OUT_WIDTH, D_MODEL), ATTN_OUT_WIDTH ** -0.5),
        "w_out": nrm(ks[8], (DEPTH, D_MODEL, D_MODEL), D_MODEL ** -0.5),
        "ffn_norm_g": 1.0 + nrm(ks[9], (DEPTH, D_MODEL), 0.02),
        "w_up": nrm(ks[10], (DEPTH, D_MODEL, 2 * D_FF), D_MODEL ** -0.5),
        "conv_w": nrm(ks[11], (DEPTH, CONV_WIDTH, D_FF), CONV_WIDTH ** -0.5),
        "conv_b": nrm(ks[12], (DEPTH, D_FF), 0.01),
        "w_down": nrm(ks[13], (DEPTH, D_FF, D_MODEL), D_FF ** -0.5),
        "final_norm_g": 1.0 + nrm(ks[14], (D_MODEL,), 0.02),
    }


def reference(x, positions, mix_norm_g, w_in, gmlp_norm_g, w_spatial, b_spatial,
              w_branch_a, w_branch_b, w_out, ffn_norm_g, w_up, conv_w, conv_b,
              w_down, final_norm_g):
    b, s, _ = x.shape
    inv_freq = ROPE_THETA ** (-jnp.arange(0, ROT_DIM, 2, dtype=jnp.float32) / ROT_DIM)
    ang = positions.astype(jnp.float32)[..., None] * inv_freq
    cos = jnp.cos(ang)[:, :, None, :]
    sin = jnp.sin(ang)[:, :, None, :]
    splits = [2 * GMLP_WIDTH, 2 * GMLP_WIDTH + ATTN_WIDTH,
              2 * GMLP_WIDTH + 2 * ATTN_WIDTH, 2 * GMLP_WIDTH + 3 * ATTN_WIDTH]
    for layer in range(DEPTH):
        h = rms_norm(x, mix_norm_g[layer])
        proj = h @ w_in[layer]
        z_a, q, k, v, gate_logits = jnp.split(proj, splits, axis=-1)
        y_a = chunked_spatial_gating(jax.nn.gelu(z_a, approximate=False),
                                     gmlp_norm_g[layer], w_spatial[layer], b_spatial[layer])
        q = apply_partial_rope(q.reshape(b, s, N_ATTN_HEADS, HEAD_DIM), cos, sin)
        k = apply_partial_rope(k.reshape(b, s, N_ATTN_HEADS, HEAD_DIM), cos, sin)
        v = v.reshape(b, s, N_ATTN_HEADS, HEAD_DIM)
        y_b = dilated_mixture_attention(q, k, v)
        gates = jax.nn.sigmoid(gate_logits.reshape(b, s, N_BRANCHES, D_MODEL))
        merged = gates[:, :, 0] * (y_a @ w_branch_a[layer]) + gates[:, :, 1] * (y_b @ w_branch_b[layer])
        x = x + merged @ w_out[layer]
        h2 = rms_norm(x, ffn_norm_g[layer])
        up = h2 @ w_up[layer]
        a, val = up[..., :D_FF], up[..., D_FF:]
        a = causal_depthwise_conv(a, conv_w[layer], conv_b[layer])
        x = x + (jax.nn.gelu(a, approximate=False) * val) @ w_down[layer]
    return rms_norm(x, final_norm_g)
```

```python
import functools

import jax
import jax.numpy as jnp
import numpy as np
from jax import lax
from jax.experimental import pallas as pl
from jax.experimental.pallas import tpu as pltpu

F32 = jnp.float32
BF16 = jnp.bfloat16

D_MODEL = 1024
EPS = 1e-6
GMLP_WIDTH = 768
GMLP_GROUPS = 4
GMLP_CHUNK = 128
HEAD_DIM = 64
HEADS_PER_GROUP = 4
GROUP_WIDTH = HEADS_PER_GROUP * HEAD_DIM
DILATIONS = (1, 4, 16)
RESIDUES_PER_STEP = (1, 1, 4)
BAND = 128
ATTN_BLOCK = 128
ROPE_THETA = 500000.0
ROT_DIM = HEAD_DIM // 4
ROT_HALF = ROT_DIM // 2
PASS_DIM = HEAD_DIM - ROT_DIM
D_FF = 2816
FF_CHUNK = 256
N_FF_CHUNKS = D_FF // FF_CHUNK

OFF_Q = 2 * GMLP_WIDTH
OFF_K = OFF_Q + 3 * GROUP_WIDTH
OFF_V = OFF_K + 3 * GROUP_WIDTH
OFF_G0 = OFF_V + 3 * GROUP_WIDTH
OFF_G1 = OFF_G0 + D_MODEL
IN_WIDTH = OFF_G1 + D_MODEL

LANES = 128
TOKEN_TILE = 512
VMEM_LIMIT_BYTES = 56 * 1024 * 1024


def _erf(x):
    return lax.erf(x)


def _gelu(x):
    return 0.5 * x * (1.0 + _erf(x * np.float32(np.sqrt(0.5))))


def _sigmoid(x):
    return 1.0 / (1.0 + jnp.exp(-x))


def _rms(x, g):
    ms = jnp.mean(x * x, axis=-1, keepdims=True)
    return x * lax.rsqrt(ms + EPS) * g


def _proj_kernel(x_ref, pos_ref, g_ref, w_ref, gg_ref, ws_ref, bias_ref, pa_ref, invf_ref,
                 q0_ref, q1_ref, q2_ref, k0_ref, k1_ref, k2_ref, v0_ref, v1_ref, v2_ref,
                 ta_ref, g1_ref):
    tm = x_ref.shape[0]
    h = _rms(x_ref[...], g_ref[...]).astype(BF16)

    def proj(lo, hi):
        return jnp.dot(h, w_ref[:, lo:hi], preferred_element_type=F32)

    z = _gelu(proj(0, 2 * GMLP_WIDTH))
    u = z[:, :GMLP_WIDTH]
    vn = _rms(z[:, GMLP_WIDTH:], gg_ref[...]).astype(BF16)
    row = lax.broadcasted_iota(jnp.int32, (GMLP_CHUNK, GMLP_CHUNK), 0)
    col = lax.broadcasted_iota(jnp.int32, (GMLP_CHUNK, GMLP_CHUNK), 1)
    tril = row >= col
    wm = [jnp.where(tril, ws_ref[g], 0.0).astype(BF16) for g in range(GMLP_GROUPS)]
    low_half = lax.broadcasted_iota(jnp.int32, (GMLP_CHUNK, LANES), 1) < (LANES // 2)
    bias = bias_ref[...]
    ya_rows = []
    for c in range(tm // GMLP_CHUNK):
        vc = vn[c * GMLP_CHUNK:(c + 1) * GMLP_CHUNK]
        r0 = jnp.dot(wm[0], vc[:, 0:256], preferred_element_type=F32)
        r1 = jnp.dot(wm[1], vc[:, 128:384], preferred_element_type=F32)
        r2 = jnp.dot(wm[2], vc[:, 384:640], preferred_element_type=F32)
        r3 = jnp.dot(wm[3], vc[:, 512:768], preferred_element_type=F32)
        mixed = jnp.concatenate(
            [r0[:, :LANES], jnp.where(low_half, r0[:, LANES:], r1[:, :LANES]), r1[:, LANES:],
             r2[:, :LANES], jnp.where(low_half, r2[:, LANES:], r3[:, :LANES]), r3[:, LANES:]],
            axis=1) + bias
        ya_rows.append(u[c * GMLP_CHUNK:(c + 1) * GMLP_CHUNK] * mixed)
    ya = jnp.concatenate(ya_rows, axis=0).astype(BF16)
    g0 = _sigmoid(proj(OFF_G0, OFF_G1))
    ta_ref[...] = (g0 * jnp.dot(ya, pa_ref[...], preferred_element_type=F32)).astype(BF16)
    g1_ref[...] = _sigmoid(proj(OFF_G1, IN_WIDTH)).astype(BF16)

    ang = pos_ref[...].astype(F32) * invf_ref[...]
    cos = jnp.cos(ang)
    sin = jnp.sin(ang)
    lane = lax.broadcasted_iota(jnp.int32, (tm, LANES), 1)
    sin_a = jnp.where(lane < 32, -sin, 0.0)
    sin_b = jnp.where(lane < 32, 0.0, sin)

    def rope_store(off, refs):
        for g, ref in enumerate(refs):
            t = proj(off + g * GROUP_WIDTH, off + (g + 1) * GROUP_WIDTH)
            t0 = t[:, :LANES]
            t0 = t0 * cos + pltpu.roll(t0, LANES - 32, 1) * sin_a + pltpu.roll(t0, 32, 1) * sin_b
            ref[:, :LANES] = t0.astype(BF16)
            ref[:, LANES:] = t[:, LANES:].astype(BF16)

    rope_store(OFF_Q, (q0_ref, q1_ref, q2_ref))
    rope_store(OFF_K, (k0_ref, k1_ref, k2_ref))
    for g, ref in enumerate((v0_ref, v1_ref, v2_ref)):
        ref[...] = proj(OFF_V + g * GROUP_WIDTH, OFF_V + (g + 1) * GROUP_WIDTH).astype(BF16)


def _resident(shape):
    nd = len(shape)
    return pl.BlockSpec(shape, lambda *_: (0,) * nd, pipeline_mode=pl.Buffered(1))


def _projection(x2, pos2, mix_g, w_in, gmlp_g, w_s, bias_tbl, w_a, inv_freq):
    n = x2.shape[0]
    tm = TOKEN_TILE
    row_spec = lambda w: pl.BlockSpec((tm, w), lambda i: (i, 0))
    grp = jax.ShapeDtypeStruct((n, GROUP_WIDTH), BF16)
    wide = jax.ShapeDtypeStruct((n, D_MODEL), BF16)
    return pl.pallas_call(
        _proj_kernel,
        grid=(n // tm,),
        in_specs=[row_spec(D_MODEL), row_spec(1), _resident((1, D_MODEL)),
                  _resident((D_MODEL, IN_WIDTH)), _resident((1, GMLP_WIDTH)),
                  _resident((GMLP_GROUPS, GMLP_CHUNK, GMLP_CHUNK)),
                  _resident((GMLP_CHUNK, GMLP_WIDTH)), _resident((GMLP_WIDTH, D_MODEL)),
                  _resident((1, LANES))],
        out_specs=[row_spec(GROUP_WIDTH)] * 9 + [row_spec(D_MODEL)] * 2,
        out_shape=[grp] * 9 + [wide] * 2,
        compiler_params=pltpu.CompilerParams(
            dimension_semantics=("arbitrary",), vmem_limit_bytes=VMEM_LIMIT_BYTES),
        name="projection",
    )(x2, pos2, mix_g, w_in, gmlp_g, w_s, bias_tbl, w_a, inv_freq)


def _head_masks():
    lane = lax.broadcasted_iota(jnp.int32, (1, GROUP_WIDTH), 1)
    qk_head = jnp.where(lane < 2 * 32, (lane % 32) // ROT_HALF, (lane - 2 * 32) // PASS_DIM)
    v_head = lane // HEAD_DIM
    return qk_head, v_head


def _attn_kernel(q_ref, k_ref, v_ref, o_ref, lse_ref, *, n_res, n_blocks):
    nh = HEADS_PER_GROUP
    blk = ATTN_BLOCK
    qk_head, v_head = _head_masks()
    scale = HEAD_DIM ** -0.5
    q_masks = [jnp.where(qk_head == h, scale, 0.0).astype(BF16) for h in range(nh)]
    v_masks = [(v_head == h).astype(F32) for h in range(nh)]
    qi = lax.broadcasted_iota(jnp.int32, (blk, 2 * blk), 0)
    ki = lax.broadcasted_iota(jnp.int32, (blk, 2 * blk), 1)
    band2 = jnp.where((ki >= qi) & (ki <= qi + BAND), 0.0, -jnp.inf).astype(F32)
    band1 = band2[:, blk:]
    band2 = jnp.concatenate([band2] * nh, axis=0)
    band1 = jnp.concatenate([band1] * nh, axis=0)

    def one_block(cols, q_rows, kv_rows, band):
        qb = q_ref[q_rows, cols]
        qs = jnp.concatenate([qb * m for m in q_masks], axis=0)
        s = lax.dot_general(qs, k_ref[kv_rows, cols], (((1,), (1,)), ((), ())),
                            preferred_element_type=F32) + band
        m = jnp.max(s, axis=-1, keepdims=True)
        p = jnp.exp(s - m)
        den = jnp.sum(p, axis=-1, keepdims=True)
        pv = jnp.dot(p.astype(BF16), v_ref[kv_rows, cols], preferred_element_type=F32)
        inv = 1.0 / den
        lse = m + jnp.log(den)
        out = jnp.zeros((blk, GROUP_WIDTH), F32)
        lse_b = jnp.zeros((blk, GROUP_WIDTH), F32)
        for h in range(nh):
            rows = slice(h * blk, (h + 1) * blk)
            out = out + (pv[rows] * inv[rows]) * v_masks[h]
            lse_b = lse_b + lse[rows] * v_masks[h]
        o_ref[q_rows, cols] = out.astype(o_ref.dtype)
        lse_ref[q_rows, cols] = lse_b

    for r in range(n_res):
        cols = slice(r * GROUP_WIDTH, (r + 1) * GROUP_WIDTH)
        one_block(cols, slice(0, blk), slice(0, blk), band1)

        def body(j, carry, cols=cols):
            q0 = pl.multiple_of(j * blk, blk)
            one_block(cols, pl.ds(q0, blk), pl.ds(q0 - blk, 2 * blk), band2)
            return carry

        lax.fori_loop(1, n_blocks, body, 0)


def _attention(q, k, v, batch, seq, dil, res_per_step):
    sub = seq // dil
    width = res_per_step * GROUP_WIDTH
    view = lambda t: t.reshape(batch, sub, dil * GROUP_WIDTH)
    spec = pl.BlockSpec((None, sub, width), lambda b, r: (b, 0, r))
    kern = functools.partial(_attn_kernel, n_res=res_per_step, n_blocks=sub // ATTN_BLOCK)
    o, lse = pl.pallas_call(
        kern,
        grid=(batch, dil // res_per_step),
        in_specs=[spec, spec, spec],
        out_specs=[spec, spec],
        out_shape=[jax.ShapeDtypeStruct((batch, sub, dil * GROUP_WIDTH), BF16),
                   jax.ShapeDtypeStruct((batch, sub, dil * GROUP_WIDTH), F32)],
        compiler_params=pltpu.CompilerParams(
            dimension_semantics=("arbitrary", "arbitrary"), vmem_limit_bytes=VMEM_LIMIT_BYTES),
        name=f"attention_d{dil}",
    )(view(q), view(k), view(v))
    return o.reshape(batch * seq, GROUP_WIDTH), lse.reshape(batch * seq, GROUP_WIDTH)


def _ffn_kernel(x_ref, ta_ref, g1_ref, o0_ref, o1_ref, o2_ref, l0_ref, l1_ref, l2_ref,
                pb_ref, wo_ref, fg_ref, wup_ref, cw_ref, cb_ref, wd_ref, og_ref,
                out_ref, carry_ref, act_ref):
    tm = x_ref.shape[0]

    @pl.when(pl.program_id(1) == 0)
    def _():
        carry_ref[...] = jnp.zeros_like(carry_ref)

    l0, l1, l2 = l0_ref[...], l1_ref[...], l2_ref[...]
    lmax = jnp.maximum(jnp.maximum(l0, l1), l2)
    e0, e1, e2 = jnp.exp(l0 - lmax), jnp.exp(l1 - lmax), jnp.exp(l2 - lmax)
    yb = (e0 * o0_ref[...].astype(F32) + e1 * o1_ref[...].astype(F32)
          + e2 * o2_ref[...].astype(F32)) / (e0 + e1 + e2)
    merged = ta_ref[...].astype(F32) + g1_ref[...].astype(F32) * jnp.dot(
        yb.astype(BF16), pb_ref[...], preferred_element_type=F32)
    x1 = x_ref[...] + jnp.dot(merged.astype(BF16), wo_ref[...], preferred_element_type=F32)
    h2 = _rms(x1, fg_ref[...]).astype(BF16)

    row8 = lax.broadcasted_iota(jnp.int32, (8, FF_CHUNK), 0)
    for c in range(N_FF_CHUNKS):
        cols = slice(c * FF_CHUNK, (c + 1) * FF_CHUNK)
        a = jnp.dot(h2, wup_ref[:, cols], preferred_element_type=F32)
        val = jnp.dot(h2, wup_ref[:, D_FF + c * FF_CHUNK:D_FF + (c + 1) * FF_CHUNK],
                      preferred_element_type=F32)
        prev = carry_ref[:, cols]
        carry_ref[:, cols] = a[tm - 8:]
        s1 = pltpu.roll(a, 1, 0)
        s2 = pltpu.roll(a, 2, 0)
        s1 = jnp.concatenate([jnp.where(row8 < 1, pltpu.roll(prev, 1, 0), s1[:8]), s1[8:]], axis=0)
        s2 = jnp.concatenate([jnp.where(row8 < 2, pltpu.roll(prev, 2, 0), s2[:8]), s2[8:]], axis=0)
        w = cw_ref[:, cols]
        y = s2 * w[0:1] + s1 * w[1:2] + a * w[2:3] + cb_ref[:, cols]
        act_ref[:, cols] = (_gelu(y) * val).astype(BF16)
    x2 = x1 + jnp.dot(act_ref[...], wd_ref[...], preferred_element_type=F32)
    out_ref[...] = _rms(x2, og_ref[...])


def _ffn(x, ta, g1, os_, ls_, w_b, w_out, ffn_g, w_up, conv_w, conv_b, w_down, final_g, batch, seq):
    tm = TOKEN_TILE
    tiles = seq // tm
    row_spec = lambda w: pl.BlockSpec((tm, w), lambda b, i: (b * tiles + i, 0))
    return pl.pallas_call(
        _ffn_kernel,
        grid=(batch, tiles),
        in_specs=[row_spec(D_MODEL), row_spec(D_MODEL), row_spec(D_MODEL)]
                 + [row_spec(GROUP_WIDTH)] * 6
                 + [_resident((GROUP_WIDTH, D_MODEL)), _resident((D_MODEL, D_MODEL)),
                    _resident((1, D_MODEL)), _resident((D_MODEL, 2 * D_FF)),
                    _resident((3, D_FF)), _resident((1, D_FF)), _resident((D_FF, D_MODEL)),
                    _resident((1, D_MODEL))],
        out_specs=row_spec(D_MODEL),
        out_shape=jax.ShapeDtypeStruct((batch * seq, D_MODEL), F32),
        scratch_shapes=[pltpu.VMEM((8, D_FF), F32), pltpu.VMEM((tm, D_FF), BF16)],
        compiler_params=pltpu.CompilerParams(
            dimension_semantics=("arbitrary", "arbitrary"), vmem_limit_bytes=VMEM_LIMIT_BYTES),
        name="mix_ffn",
    )(x, ta, g1, *os_, *ls_, w_b, w_out, ffn_g, w_up, conv_w, conv_b, w_down, final_g)


def _rotary_first(w):
    d = w.shape[0]
    t = w.reshape(d, 3, HEADS_PER_GROUP, HEAD_DIM)
    parts = [t[..., :ROT_HALF].reshape(d, 3, HEADS_PER_GROUP * ROT_HALF),
             t[..., ROT_HALF:ROT_DIM].reshape(d, 3, HEADS_PER_GROUP * ROT_HALF),
             t[..., ROT_DIM:].reshape(d, 3, HEADS_PER_GROUP * PASS_DIM)]
    return jnp.concatenate(parts, axis=-1).reshape(d, 3 * GROUP_WIDTH)


def kernel(x, positions, mix_norm_g, w_in, gmlp_norm_g, w_spatial, b_spatial, w_branch_a,
           w_branch_b, w_out, ffn_norm_g, w_up, conv_w, conv_b, w_down, final_norm_g):
    batch, seq, d = x.shape
    assert w_in.shape[0] == 1, "single-layer block: the final norm is fused into the FFN stage"
    layer = 0
    n = batch * seq
    inv_freq = ROPE_THETA ** (-jnp.arange(0, ROT_DIM, 2, dtype=F32) / ROT_DIM)
    inv_freq = jnp.concatenate(
        [jnp.tile(inv_freq, 2 * HEADS_PER_GROUP), jnp.zeros((LANES - 2 * 32,), F32)]).reshape(1, LANES)
    pos2 = positions.reshape(n, 1)
    xf = x.reshape(n, d)
    wi = w_in[layer]
    wi = jnp.concatenate([wi[:, :OFF_Q], _rotary_first(wi[:, OFF_Q:OFF_K]),
                          _rotary_first(wi[:, OFF_K:OFF_V]), wi[:, OFF_V:]], axis=1).astype(BF16)
    bias_tbl = jnp.repeat(b_spatial[layer].T, GMLP_WIDTH // GMLP_GROUPS, axis=1)
    res = _projection(xf, pos2, mix_norm_g[layer].reshape(1, d), wi,
                      gmlp_norm_g[layer].reshape(1, GMLP_WIDTH), w_spatial[layer], bias_tbl,
                      w_branch_a[layer].astype(BF16), inv_freq)
    qs, ks, vs, ta, g1 = res[0:3], res[3:6], res[6:9], res[9], res[10]
    outs, lses = [], []
    for g, (dil, rps) in enumerate(zip(DILATIONS, RESIDUES_PER_STEP)):
        o, lse = _attention(qs[g], ks[g], vs[g], batch, seq, dil, rps)
        outs.append(o)
        lses.append(lse)
    out = _ffn(xf, ta, g1, outs, lses, w_branch_b[layer].astype(BF16), w_out[layer].astype(BF16),
               ffn_norm_g[layer].reshape(1, d), w_up[layer].astype(BF16), conv_w[layer],
               conv_b[layer].reshape(1, D_FF), w_down[layer].astype(BF16),
               final_norm_g.reshape(1, d), batch, seq)
    return out.reshape(batch, seq, d)
```

```python
import functools

import jax
import jax.numpy as jnp
import numpy as np
from jax import lax
from jax.experimental import pallas as pl
from jax.experimental.pallas import tpu as pltpu

F32 = jnp.float32
BF16 = jnp.bfloat16

D_MODEL = 1024
EPS = 1e-6
GMLP_WIDTH = 768
GMLP_GROUPS = 4
GMLP_CHUNK = 128
HEAD_DIM = 64
HEADS_PER_GROUP = 4
GROUP_WIDTH = HEADS_PER_GROUP * HEAD_DIM
DILATIONS = (1, 4, 16)
RESIDUES_PER_STEP = (1, 1, 4)
BAND = 128
ATTN_BLOCK = 128
ROPE_THETA = 500000.0
ROT_DIM = HEAD_DIM // 4
ROT_HALF = ROT_DIM // 2
PASS_DIM = HEAD_DIM - ROT_DIM
D_FF = 2816
FF_CHUNK = 256
N_FF_CHUNKS = D_FF // FF_CHUNK

OFF_Q = 2 * GMLP_WIDTH
OFF_K = OFF_Q + 3 * GROUP_WIDTH
OFF_V = OFF_K + 3 * GROUP_WIDTH
OFF_G0 = OFF_V + 3 * GROUP_WIDTH
OFF_G1 = OFF_G0 + D_MODEL
IN_WIDTH = OFF_G1 + D_MODEL

LANES = 128
TOKEN_TILE = 512
VMEM_LIMIT_BYTES = 56 * 1024 * 1024


def _erf(x):
    return lax.erf(x)


def _gelu(x):
    return 0.5 * x * (1.0 + _erf(x * np.float32(np.sqrt(0.5))))


def _sigmoid(x):
    return 1.0 / (1.0 + jnp.exp(-x))


def _rms(x, g):
    ms = jnp.mean(x * x, axis=-1, keepdims=True)
    return x * lax.rsqrt(ms + EPS) * g


def _proj_kernel(x_ref, pos_ref, g_ref, w_ref, gg_ref, ws_ref, bias_ref, pa_ref, invf_ref,
                 q0_ref, q1_ref, q2_ref, k0_ref, k1_ref, k2_ref, v0_ref, v1_ref, v2_ref,
                 ta_ref, g1_ref, stage_ref):
    tm = x_ref.shape[0]
    h = _rms(x_ref[...], g_ref[...]).astype(BF16)

    def proj(lo, hi):
        return jnp.dot(h, w_ref[:, lo:hi], preferred_element_type=F32)

    z = _gelu(proj(0, 2 * GMLP_WIDTH))
    u = z[:, :GMLP_WIDTH]
    vn = _rms(z[:, GMLP_WIDTH:], gg_ref[...]).astype(BF16)
    row = lax.broadcasted_iota(jnp.int32, (GMLP_CHUNK, GMLP_CHUNK), 0)
    col = lax.broadcasted_iota(jnp.int32, (GMLP_CHUNK, GMLP_CHUNK), 1)
    tril = row >= col
    wm = [jnp.where(tril, ws_ref[g], 0.0).astype(BF16) for g in range(GMLP_GROUPS)]
    low_half = lax.broadcasted_iota(jnp.int32, (GMLP_CHUNK, LANES), 1) < (LANES // 2)
    bias = bias_ref[...]
    ya_rows = []
    for c in range(tm // GMLP_CHUNK):
        vc = vn[c * GMLP_CHUNK:(c + 1) * GMLP_CHUNK]
        r0 = jnp.dot(wm[0], vc[:, 0:256], preferred_element_type=F32)
        r1 = jnp.dot(wm[1], vc[:, 128:384], preferred_element_type=F32)
        r2 = jnp.dot(wm[2], vc[:, 384:640], preferred_element_type=F32)
        r3 = jnp.dot(wm[3], vc[:, 512:768], preferred_element_type=F32)
        mixed = jnp.concatenate(
            [r0[:, :LANES], jnp.where(low_half, r0[:, LANES:], r1[:, :LANES]), r1[:, LANES:],
             r2[:, :LANES], jnp.where(low_half, r2[:, LANES:], r3[:, :LANES]), r3[:, LANES:]],
            axis=1) + bias
        ya_rows.append(u[c * GMLP_CHUNK:(c + 1) * GMLP_CHUNK] * mixed)
    ya = jnp.concatenate(ya_rows, axis=0).astype(BF16)
    g0 = _sigmoid(proj(OFF_G0, OFF_G1))
    ta_ref[...] = (g0 * jnp.dot(ya, pa_ref[...], preferred_element_type=F32)).astype(BF16)
    g1_ref[...] = _sigmoid(proj(OFF_G1, IN_WIDTH)).astype(BF16)

    ang = pos_ref[...].astype(F32) * invf_ref[...]
    cos = jnp.cos(ang)
    sin = jnp.sin(ang)
    lane = lax.broadcasted_iota(jnp.int32, (tm, LANES), 1)
    sin_a = jnp.where(lane < 32, -sin, 0.0)
    sin_b = jnp.where(lane < 32, 0.0, sin)

    def store_by_residue(halves, ref, g, slot):
        dil = DILATIONS[g]
        if dil == 1:
            for s, half in enumerate(halves):
                ref[:, s * LANES:(s + 1) * LANES] = half.astype(BF16)
            return
        for s, half in enumerate(halves):
            stage_ref[slot, s] = half
        for r in range(dil):
            for s in range(2):
                lo = r * GROUP_WIDTH + s * LANES
                ref[:, lo:lo + LANES] = stage_ref[slot, s, pl.ds(r, tm // dil, stride=dil), :].astype(BF16)

    def rope_store(off, refs, slot0):
        for g, ref in enumerate(refs):
            t = proj(off + g * GROUP_WIDTH, off + (g + 1) * GROUP_WIDTH)
            t0 = t[:, :LANES]
            t0 = t0 * cos + pltpu.roll(t0, LANES - 32, 1) * sin_a + pltpu.roll(t0, 32, 1) * sin_b
            store_by_residue((t0, t[:, LANES:]), ref, g, slot0 + g - 1)

    rope_store(OFF_Q, (q0_ref, q1_ref, q2_ref), 0)
    rope_store(OFF_K, (k0_ref, k1_ref, k2_ref), 2)
    for g, ref in enumerate((v0_ref, v1_ref, v2_ref)):
        t = proj(OFF_V + g * GROUP_WIDTH, OFF_V + (g + 1) * GROUP_WIDTH)
        store_by_residue((t[:, :LANES], t[:, LANES:]), ref, g, 4 + g - 1)


def _resident(shape):
    nd = len(shape)
    return pl.BlockSpec(shape, lambda *_: (0,) * nd, pipeline_mode=pl.Buffered(1))


def _projection(x2, pos2, mix_g, w_in, gmlp_g, w_s, bias_tbl, w_a, inv_freq):
    n = x2.shape[0]
    tm = TOKEN_TILE
    row_spec = lambda w: pl.BlockSpec((tm, w), lambda i: (i, 0))
    grp_specs = [pl.BlockSpec((tm // d, d * GROUP_WIDTH), lambda i: (i, 0)) for d in DILATIONS] * 3
    grp_shapes = [jax.ShapeDtypeStruct((n // d, d * GROUP_WIDTH), BF16) for d in DILATIONS] * 3
    wide = jax.ShapeDtypeStruct((n, D_MODEL), BF16)
    return pl.pallas_call(
        _proj_kernel,
        grid=(n // tm,),
        in_specs=[row_spec(D_MODEL), row_spec(1), _resident((1, D_MODEL)),
                  _resident((D_MODEL, IN_WIDTH)), _resident((1, GMLP_WIDTH)),
                  _resident((GMLP_GROUPS, GMLP_CHUNK, GMLP_CHUNK)),
                  _resident((GMLP_CHUNK, GMLP_WIDTH)), _resident((GMLP_WIDTH, D_MODEL)),
                  _resident((1, LANES))],
        out_specs=grp_specs + [row_spec(D_MODEL)] * 2,
        out_shape=grp_shapes + [wide] * 2,
        scratch_shapes=[pltpu.VMEM((6, 2, tm, LANES), F32)],
        compiler_params=pltpu.CompilerParams(
            dimension_semantics=("arbitrary",), vmem_limit_bytes=VMEM_LIMIT_BYTES),
        name="projection",
    )(x2, pos2, mix_g, w_in, gmlp_g, w_s, bias_tbl, w_a, inv_freq)


def _head_masks():
    lane = lax.broadcasted_iota(jnp.int32, (1, GROUP_WIDTH), 1)
    qk_head = jnp.where(lane < 2 * 32, (lane % 32) // ROT_HALF, (lane - 2 * 32) // PASS_DIM)
    v_head = lane // HEAD_DIM
    return qk_head, v_head


def _attn_kernel(q_ref, k_ref, v_ref, o_ref, lse_ref, *, n_res, n_blocks):
    nh = HEADS_PER_GROUP
    blk = ATTN_BLOCK
    qk_head, v_head = _head_masks()
    scale = HEAD_DIM ** -0.5
    q_masks = [jnp.where(qk_head == h, scale, 0.0).astype(BF16) for h in range(nh)]
    v_masks = [(v_head == h).astype(F32) for h in range(nh)]
    qi = lax.broadcasted_iota(jnp.int32, (blk, 2 * blk), 0)
    ki = lax.broadcasted_iota(jnp.int32, (blk, 2 * blk), 1)
    band2 = jnp.where((ki >= qi) & (ki <= qi + BAND), 0.0, -jnp.inf).astype(F32)
    band1 = band2[:, blk:]
    band2 = jnp.concatenate([band2] * nh, axis=0)
    band1 = jnp.concatenate([band1] * nh, axis=0)

    def one_block(cols, q_rows, kv_rows, band):
        qb = q_ref[q_rows, cols]
        qs = jnp.concatenate([qb * m for m in q_masks], axis=0)
        s = lax.dot_general(qs, k_ref[kv_rows, cols], (((1,), (1,)), ((), ())),
                            preferred_element_type=F32) + band
        m = jnp.max(s, axis=-1, keepdims=True)
        p = jnp.exp(s - m)
        den = jnp.sum(p, axis=-1, keepdims=True)
        pv = jnp.dot(p.astype(BF16), v_ref[kv_rows, cols], preferred_element_type=F32)
        inv = 1.0 / den
        lse = m + jnp.log(den)
        out = jnp.zeros((blk, GROUP_WIDTH), F32)
        lse_b = jnp.zeros((blk, GROUP_WIDTH), F32)
        for h in range(nh):
            rows = slice(h * blk, (h + 1) * blk)
            out = out + (pv[rows] * inv[rows]) * v_masks[h]
            lse_b = lse_b + lse[rows] * v_masks[h]
        o_ref[q_rows, cols] = out.astype(o_ref.dtype)
        lse_ref[q_rows, cols] = lse_b

    for r in range(n_res):
        cols = slice(r * GROUP_WIDTH, (r + 1) * GROUP_WIDTH)
        one_block(cols, slice(0, blk), slice(0, blk), band1)

        def body(j, carry, cols=cols):
            q0 = pl.multiple_of(j * blk, blk)
            one_block(cols, pl.ds(q0, blk), pl.ds(q0 - blk, 2 * blk), band2)
            return carry

        lax.fori_loop(1, n_blocks, body, 0)


def _attention(q, k, v, batch, seq, dil, res_per_step):
    sub = seq // dil
    width = res_per_step * GROUP_WIDTH
    view = lambda t: t.reshape(batch, sub, dil * GROUP_WIDTH)
    spec = pl.BlockSpec((None, sub, width), lambda b, r: (b, 0, r))
    kern = functools.partial(_attn_kernel, n_res=res_per_step, n_blocks=sub // ATTN_BLOCK)
    o, lse = pl.pallas_call(
        kern,
        grid=(batch, dil // res_per_step),
        in_specs=[spec, spec, spec],
        out_specs=[spec, spec],
        out_shape=[jax.ShapeDtypeStruct((batch, sub, dil * GROUP_WIDTH), BF16),
                   jax.ShapeDtypeStruct((batch, sub, dil * GROUP_WIDTH), F32)],
        compiler_params=pltpu.CompilerParams(
            dimension_semantics=("arbitrary", "arbitrary"), vmem_limit_bytes=VMEM_LIMIT_BYTES),
        name=f"attention_d{dil}",
    )(view(q), view(k), view(v))
    return o.reshape(batch * sub, dil * GROUP_WIDTH), lse.reshape(batch * sub, dil * GROUP_WIDTH)


def _ffn_kernel(x_ref, ta_ref, g1_ref, o0_ref, o1_ref, o2_ref, l0_ref, l1_ref, l2_ref,
                pb_ref, wo_ref, fg_ref, wup_ref, cw_ref, cb_ref, wd_ref, og_ref,
                out_ref, carry_ref, act_ref, stage_ref):
    tm = x_ref.shape[0]

    @pl.when(pl.program_id(1) == 0)
    def _():
        carry_ref[...] = jnp.zeros_like(carry_ref)

    def natural(ref, g, slot):
        dil = DILATIONS[g]
        if dil == 1:
            return ref[...].astype(F32)
        for r in range(dil):
            for s in range(2):
                lo = r * GROUP_WIDTH + s * LANES
                stage_ref[slot, s, pl.ds(r, tm // dil, stride=dil), :] = ref[:, lo:lo + LANES].astype(F32)
        return jnp.concatenate([stage_ref[slot, 0], stage_ref[slot, 1]], axis=1)

    l0, l1, l2 = l0_ref[...], natural(l1_ref, 1, 0), natural(l2_ref, 2, 1)
    lmax = jnp.maximum(jnp.maximum(l0, l1), l2)
    e0, e1, e2 = jnp.exp(l0 - lmax), jnp.exp(l1 - lmax), jnp.exp(l2 - lmax)
    yb = (e0 * o0_ref[...].astype(F32) + e1 * natural(o1_ref, 1, 2)
          + e2 * natural(o2_ref, 2, 3)) / (e0 + e1 + e2)
    merged = ta_ref[...].astype(F32) + g1_ref[...].astype(F32) * jnp.dot(
        yb.astype(BF16), pb_ref[...], preferred_element_type=F32)
    x1 = x_ref[...] + jnp.dot(merged.astype(BF16), wo_ref[...], preferred_element_type=F32)
    h2 = _rms(x1, fg_ref[...]).astype(BF16)

    row8 = lax.broadcasted_iota(jnp.int32, (8, FF_CHUNK), 0)
    for c in range(N_FF_CHUNKS):
        cols = slice(c * FF_CHUNK, (c + 1) * FF_CHUNK)
        a = jnp.dot(h2, wup_ref[:, cols], preferred_element_type=F32)
        val = jnp.dot(h2, wup_ref[:, D_FF + c * FF_CHUNK:D_FF + (c + 1) * FF_CHUNK],
                      preferred_element_type=F32)
        prev = carry_ref[:, cols]
        carry_ref[:, cols] = a[tm - 8:]
        s1 = pltpu.roll(a, 1, 0)
        s2 = pltpu.roll(a, 2, 0)
        s1 = jnp.concatenate([jnp.where(row8 < 1, pltpu.roll(prev, 1, 0), s1[:8]), s1[8:]], axis=0)
        s2 = jnp.concatenate([jnp.where(row8 < 2, pltpu.roll(prev, 2, 0), s2[:8]), s2[8:]], axis=0)
        w = cw_ref[:, cols]
        y = s2 * w[0:1] + s1 * w[1:2] + a * w[2:3] + cb_ref[:, cols]
        act_ref[:, cols] = (_gelu(y) * val).astype(BF16)
    x2 = x1 + jnp.dot(act_ref[...], wd_ref[...], preferred_element_type=F32)
    out_ref[...] = _rms(x2, og_ref[...])


def _ffn(x, ta, g1, os_, ls_, w_b, w_out, ffn_g, w_up, conv_w, conv_b, w_down, final_g, batch, seq):
    tm = TOKEN_TILE
    tiles = seq // tm
    row_spec = lambda w: pl.BlockSpec((tm, w), lambda b, i: (b * tiles + i, 0))
    grp_specs = [pl.BlockSpec((tm // d, d * GROUP_WIDTH), lambda b, i: (b * tiles + i, 0))
                 for d in DILATIONS]
    return pl.pallas_call(
        _ffn_kernel,
        grid=(batch, tiles),
        in_specs=[row_spec(D_MODEL), row_spec(D_MODEL), row_spec(D_MODEL)]
                 + grp_specs * 2
                 + [_resident((GROUP_WIDTH, D_MODEL)), _resident((D_MODEL, D_MODEL)),
                    _resident((1, D_MODEL)), _resident((D_MODEL, 2 * D_FF)),
                    _resident((3, D_FF)), _resident((1, D_FF)), _resident((D_FF, D_MODEL)),
                    _resident((1, D_MODEL))],
        out_specs=row_spec(D_MODEL),
        out_shape=jax.ShapeDtypeStruct((batch * seq, D_MODEL), F32),
        scratch_shapes=[pltpu.VMEM((8, D_FF), F32), pltpu.VMEM((tm, D_FF), BF16),
                        pltpu.VMEM((4, 2, tm, LANES), F32)],
        compiler_params=pltpu.CompilerParams(
            dimension_semantics=("arbitrary", "arbitrary"), vmem_limit_bytes=VMEM_LIMIT_BYTES),
        name="mix_ffn",
    )(x, ta, g1, *os_, *ls_, w_b, w_out, ffn_g, w_up, conv_w, conv_b, w_down, final_g)


def _rotary_first(w):
    d = w.shape[0]
    t = w.reshape(d, 3, HEADS_PER_GROUP, HEAD_DIM)
    parts = [t[..., :ROT_HALF].reshape(d, 3, HEADS_PER_GROUP * ROT_HALF),
             t[..., ROT_HALF:ROT_DIM].reshape(d, 3, HEADS_PER_GROUP * ROT_HALF),
             t[..., ROT_DIM:].reshape(d, 3, HEADS_PER_GROUP * PASS_DIM)]
    return jnp.concatenate(parts, axis=-1).reshape(d, 3 * GROUP_WIDTH)


def kernel(x, positions, mix_norm_g, w_in, gmlp_norm_g, w_spatial, b_spatial, w_branch_a,
           w_branch_b, w_out, ffn_norm_g, w_up, conv_w, conv_b, w_down, final_norm_g):
    batch, seq, d = x.shape
    assert w_in.shape[0] == 1, "single-layer block: the final norm is fused into the FFN stage"
    layer = 0
    n = batch * seq
    inv_freq = ROPE_THETA ** (-jnp.arange(0, ROT_DIM, 2, dtype=F32) / ROT_DIM)
    inv_freq = jnp.concatenate(
        [jnp.tile(inv_freq, 2 * HEADS_PER_GROUP), jnp.zeros((LANES - 2 * 32,), F32)]).reshape(1, LANES)
    pos2 = positions.reshape(n, 1)
    xf = x.reshape(n, d)
    wi = w_in[layer]
    wi = jnp.concatenate([wi[:, :OFF_Q], _rotary_first(wi[:, OFF_Q:OFF_K]),
                          _rotary_first(wi[:, OFF_K:OFF_V]), wi[:, OFF_V:]], axis=1).astype(BF16)
    bias_tbl = jnp.repeat(b_spatial[layer].T, GMLP_WIDTH // GMLP_GROUPS, axis=1)
    res = _projection(xf, pos2, mix_norm_g[layer].reshape(1, d), wi,
                      gmlp_norm_g[layer].reshape(1, GMLP_WIDTH), w_spatial[layer], bias_tbl,
                      w_branch_a[layer].astype(BF16), inv_freq)
    qs, ks, vs, ta, g1 = res[0:3], res[3:6], res[6:9], res[9], res[10]
    outs, lses = [], []
    for g, (dil, rps) in enumerate(zip(DILATIONS, RESIDUES_PER_STEP)):
        o, lse = _attention(qs[g], ks[g], vs[g], batch, seq, dil, rps)
        outs.append(o)
        lses.append(lse)
    out = _ffn(xf, ta, g1, outs, lses, w_branch_b[layer].astype(BF16), w_out[layer].astype(BF16),
               ffn_norm_g[layer].reshape(1, d), w_up[layer].astype(BF16), conv_w[layer],
               conv_b[layer].reshape(1, D_FF), w_down[layer].astype(BF16),
               final_norm_g.reshape(1, d), batch, seq)
    return out.reshape(batch, seq, d)
```

```python
import functools

import jax
import jax.numpy as jnp
import numpy as np
from jax import lax
from jax.experimental import pallas as pl
from jax.experimental.pallas import tpu as pltpu

F32 = jnp.float32
BF16 = jnp.bfloat16

D_MODEL = 1024
EPS = 1e-6
GMLP_WIDTH = 768
GMLP_GROUPS = 4
GMLP_CHUNK = 128
HEAD_DIM = 64
HEADS_PER_GROUP = 4
GROUP_WIDTH = HEADS_PER_GROUP * HEAD_DIM
DILATIONS = (1, 4, 16)
RESIDUES_PER_STEP = (1, 1, 4)
BAND = 128
ATTN_BLOCK = 128
ROPE_THETA = 500000.0
ROT_DIM = HEAD_DIM // 4
ROT_HALF = ROT_DIM // 2
PASS_DIM = HEAD_DIM - ROT_DIM
D_FF = 2816
FF_CHUNK = 256
N_FF_CHUNKS = D_FF // FF_CHUNK

OFF_Q = 2 * GMLP_WIDTH
OFF_K = OFF_Q + 3 * GROUP_WIDTH
OFF_V = OFF_K + 3 * GROUP_WIDTH
OFF_G0 = OFF_V + 3 * GROUP_WIDTH
OFF_G1 = OFF_G0 + D_MODEL
IN_WIDTH = OFF_G1 + D_MODEL

LANES = 128
PROJ_TILE = 512
PROJ_SUBTILE = 256
FFN_TILE = 512
FFN_SUBTILE = 256
VMEM_LIMIT_BYTES = 56 * 1024 * 1024


def _erf(x):
    return lax.erf(x)


def _gelu(x):
    return 0.5 * x * (1.0 + _erf(x * np.float32(np.sqrt(0.5))))


def _sigmoid(x):
    return 1.0 / (1.0 + jnp.exp(-x))


def _rms(x, g):
    ms = jnp.mean(x * x, axis=-1, keepdims=True)
    return x * lax.rsqrt(ms + EPS) * g


def _proj_kernel(x_ref, pos_ref, g_ref, w_ref, gg_ref, ws_ref, bias_ref, pa_ref, invf_ref,
                 q0_ref, q1_ref, q2_ref, k0_ref, k1_ref, k2_ref, v0_ref, v1_ref, v2_ref,
                 ta_ref, g1_ref, stage_ref):
    tm = x_ref.shape[0]
    rows = PROJ_SUBTILE
    subs = range(tm // rows)
    q_refs, k_refs, v_refs = (q0_ref, q1_ref, q2_ref), (k0_ref, k1_ref, k2_ref), (v0_ref, v1_ref, v2_ref)
    row = lax.broadcasted_iota(jnp.int32, (GMLP_CHUNK, GMLP_CHUNK), 0)
    col = lax.broadcasted_iota(jnp.int32, (GMLP_CHUNK, GMLP_CHUNK), 1)
    wm = [jnp.where(row >= col, ws_ref[g], 0.0).astype(BF16) for g in range(GMLP_GROUPS)]
    low_half = lax.broadcasted_iota(jnp.int32, (GMLP_CHUNK, LANES), 1) < (LANES // 2)

    def proj(h, lo, hi):
        return jnp.dot(h, w_ref[:, lo:hi], preferred_element_type=F32)

    def gating_inputs(sub):
        h = _rms(x_ref[sub * rows:(sub + 1) * rows], g_ref[...]).astype(BF16)
        vn = _rms(_gelu(proj(h, GMLP_WIDTH, 2 * GMLP_WIDTH)), gg_ref[...]).astype(BF16)
        u = _gelu(proj(h, 0, GMLP_WIDTH))
        return h, u, vn

    def store_by_residue(sub, halves, ref, g, slot):
        dil = DILATIONS[g]
        out_rows = slice(sub * rows // dil, (sub + 1) * rows // dil)
        if dil == 1:
            for s, half in enumerate(halves):
                ref[out_rows, s * LANES:(s + 1) * LANES] = half.astype(BF16)
            return
        for s, half in enumerate(halves):
            stage_ref[sub, slot, s] = half
        for r in range(dil):
            for s in range(2):
                lo = r * GROUP_WIDTH + s * LANES
                ref[out_rows, lo:lo + LANES] = stage_ref[
                    sub, slot, s, pl.ds(r, rows // dil, stride=dil), :].astype(BF16)

    def gates_and_qkv(sub, h):
        g0 = _sigmoid(proj(h, OFF_G0, OFF_G1))
        g1_ref[sub * rows:(sub + 1) * rows] = _sigmoid(proj(h, OFF_G1, IN_WIDTH)).astype(BF16)
        pos = pos_ref[:, sub * rows:(sub + 1) * rows].astype(F32)
        ang = jnp.concatenate([invf_ref[...] * pos[:, b * LANES:(b + 1) * LANES]
                               for b in range(rows // LANES)], axis=1)
        n_rot = 2 * HEADS_PER_GROUP
        cos = jnp.concatenate([jnp.cos(ang)] * n_rot + [jnp.ones((LANES - 8 * n_rot, rows), F32)], axis=0).T
        sin = jnp.concatenate([jnp.sin(ang)] * n_rot + [jnp.zeros((LANES - 8 * n_rot, rows), F32)], axis=0).T
        lane = lax.broadcasted_iota(jnp.int32, (rows, LANES), 1)
        sin_a = jnp.where(lane < 32, -sin, 0.0)
        sin_b = jnp.where(lane < 32, 0.0, sin)
        for off, refs, slot0 in ((OFF_Q, q_refs, 0), (OFF_K, k_refs, 2)):
            for g, ref in enumerate(refs):
                t = proj(h, off + g * GROUP_WIDTH, off + (g + 1) * GROUP_WIDTH)
                t0 = t[:, :LANES]
                t0 = t0 * cos + pltpu.roll(t0, LANES - 32, 1) * sin_a + pltpu.roll(t0, 32, 1) * sin_b
                store_by_residue(sub, (t0, t[:, LANES:]), ref, g, slot0 + g - 1)
        for g, ref in enumerate(v_refs):
            t = proj(h, OFF_V + g * GROUP_WIDTH, OFF_V + (g + 1) * GROUP_WIDTH)
            store_by_residue(sub, (t[:, :LANES], t[:, LANES:]), ref, g, 4 + g - 1)
        return g0

    def spatial_gating(sub, u, vn, g0):
        bias = bias_ref[...]
        ya_rows = []
        for c in range(rows // GMLP_CHUNK):
            vc = vn[c * GMLP_CHUNK:(c + 1) * GMLP_CHUNK]
            p0 = jnp.dot(wm[0], vc[:, 0:256], preferred_element_type=F32)
            p1 = jnp.dot(wm[1], vc[:, 128:384], preferred_element_type=F32)
            p2 = jnp.dot(wm[2], vc[:, 384:640], preferred_element_type=F32)
            p3 = jnp.dot(wm[3], vc[:, 512:768], preferred_element_type=F32)
            mixed = jnp.concatenate(
                [p0[:, :LANES], jnp.where(low_half, p0[:, LANES:], p1[:, :LANES]), p1[:, LANES:],
                 p2[:, :LANES], jnp.where(low_half, p2[:, LANES:], p3[:, :LANES]), p3[:, LANES:]],
                axis=1) + bias
            ya_rows.append(u[c * GMLP_CHUNK:(c + 1) * GMLP_CHUNK] * mixed)
        ya = jnp.concatenate(ya_rows, axis=0).astype(BF16)
        ta_ref[sub * rows:(sub + 1) * rows] = (
            g0 * jnp.dot(ya, pa_ref[...], preferred_element_type=F32)).astype(BF16)

    chains = [gating_inputs(sub) for sub in subs]
    gates = [gates_and_qkv(sub, chains[sub][0]) for sub in subs]
    for sub in subs:
        spatial_gating(sub, chains[sub][1], chains[sub][2], gates[sub])


def _resident(shape):
    nd = len(shape)
    return pl.BlockSpec(shape, lambda *_: (0,) * nd, pipeline_mode=pl.Buffered(1))


def _projection(x2, pos2, mix_g, w_in, gmlp_g, w_s, bias_tbl, w_a, inv_freq):
    n = x2.shape[0]
    tm = PROJ_TILE
    row_spec = lambda w: pl.BlockSpec((tm, w), lambda i: (i, 0))
    grp_specs = [pl.BlockSpec((tm // d, d * GROUP_WIDTH), lambda i: (i, 0)) for d in DILATIONS] * 3
    grp_shapes = [jax.ShapeDtypeStruct((n // d, d * GROUP_WIDTH), BF16) for d in DILATIONS] * 3
    wide = jax.ShapeDtypeStruct((n, D_MODEL), BF16)
    return pl.pallas_call(
        _proj_kernel,
        grid=(n // tm,),
        in_specs=[row_spec(D_MODEL), pl.BlockSpec((None, 1, tm), lambda i: (i, 0, 0)),
                  _resident((1, D_MODEL)),
                  _resident((D_MODEL, IN_WIDTH)), _resident((1, GMLP_WIDTH)),
                  _resident((GMLP_GROUPS, GMLP_CHUNK, GMLP_CHUNK)),
                  _resident((GMLP_CHUNK, GMLP_WIDTH)), _resident((GMLP_WIDTH, D_MODEL)),
                  _resident((ROT_HALF, LANES))],
        out_specs=grp_specs + [row_spec(D_MODEL)] * 2,
        out_shape=grp_shapes + [wide] * 2,
        scratch_shapes=[pltpu.VMEM((tm // PROJ_SUBTILE, 6, 2, PROJ_SUBTILE, LANES), F32)],
        compiler_params=pltpu.CompilerParams(
            dimension_semantics=("arbitrary",), vmem_limit_bytes=VMEM_LIMIT_BYTES),
        name="projection",
    )(x2, pos2, mix_g, w_in, gmlp_g, w_s, bias_tbl, w_a, inv_freq)


def _head_masks():
    lane = lax.broadcasted_iota(jnp.int32, (1, GROUP_WIDTH), 1)
    qk_head = jnp.where(lane < 2 * 32, (lane % 32) // ROT_HALF, (lane - 2 * 32) // PASS_DIM)
    v_head = lane // HEAD_DIM
    return qk_head, v_head


def _attn_kernel(q_ref, k_ref, v_ref, o_ref, lse_ref, *, n_res, n_blocks):
    nh = HEADS_PER_GROUP
    blk = ATTN_BLOCK
    qk_head, v_head = _head_masks()
    scale = HEAD_DIM ** -0.5
    q_masks = [jnp.where(qk_head == h, scale, 0.0).astype(BF16) for h in range(nh)]
    v_masks = [(v_head == h).astype(F32) for h in range(nh)]
    qi = lax.broadcasted_iota(jnp.int32, (blk, 2 * blk), 0)
    ki = lax.broadcasted_iota(jnp.int32, (blk, 2 * blk), 1)
    band2 = jnp.where((ki >= qi) & (ki <= qi + BAND), 0.0, -jnp.inf).astype(F32)
    band1 = band2[:, blk:]
    band2 = jnp.concatenate([band2] * nh, axis=0)
    band1 = jnp.concatenate([band1] * nh, axis=0)

    def one_block(cols, q_rows, kv_rows, band):
        qb = q_ref[q_rows, cols]
        qs = jnp.concatenate([qb * m for m in q_masks], axis=0)
        s = lax.dot_general(qs, k_ref[kv_rows, cols], (((1,), (1,)), ((), ())),
                            preferred_element_type=F32) + band
        m = jnp.max(s, axis=-1, keepdims=True)
        p = jnp.exp(s - m)
        den = jnp.sum(p, axis=-1, keepdims=True)
        pv = jnp.dot(p.astype(BF16), v_ref[kv_rows, cols], preferred_element_type=F32)
        inv = 1.0 / den
        lse = m + jnp.log(den)
        out = jnp.zeros((blk, GROUP_WIDTH), F32)
        lse_b = jnp.zeros((blk, GROUP_WIDTH), F32)
        for h in range(nh):
            rows = slice(h * blk, (h + 1) * blk)
            out = out + (pv[rows] * inv[rows]) * v_masks[h]
            lse_b = lse_b + lse[rows] * v_masks[h]
        o_ref[q_rows, cols] = out.astype(o_ref.dtype)
        lse_ref[q_rows, cols] = lse_b

    for r in range(n_res):
        cols = slice(r * GROUP_WIDTH, (r + 1) * GROUP_WIDTH)
        one_block(cols, slice(0, blk), slice(0, blk), band1)

        def body(j, carry, cols=cols):
            q0 = pl.multiple_of(j * blk, blk)
            one_block(cols, pl.ds(q0, blk), pl.ds(q0 - blk, 2 * blk), band2)
            return carry

        lax.fori_loop(1, n_blocks, body, 0)


def _attention(q, k, v, batch, seq, dil, res_per_step):
    sub = seq // dil
    width = res_per_step * GROUP_WIDTH
    view = lambda t: t.reshape(batch, sub, dil * GROUP_WIDTH)
    spec = pl.BlockSpec((None, sub, width), lambda b, r: (b, 0, r))
    kern = functools.partial(_attn_kernel, n_res=res_per_step, n_blocks=sub // ATTN_BLOCK)
    o, lse = pl.pallas_call(
        kern,
        grid=(batch, dil // res_per_step),
        in_specs=[spec, spec, spec],
        out_specs=[spec, spec],
        out_shape=[jax.ShapeDtypeStruct((batch, sub, dil * GROUP_WIDTH), BF16),
                   jax.ShapeDtypeStruct((batch, sub, dil * GROUP_WIDTH), F32)],
        compiler_params=pltpu.CompilerParams(
            dimension_semantics=("arbitrary", "arbitrary"), vmem_limit_bytes=VMEM_LIMIT_BYTES),
        name=f"attention_d{dil}",
    )(view(q), view(k), view(v))
    return o.reshape(batch * sub, dil * GROUP_WIDTH), lse.reshape(batch * sub, dil * GROUP_WIDTH)


def _ffn_kernel(x_ref, ta_ref, g1_ref, o0_ref, o1_ref, o2_ref, l0_ref, l1_ref, l2_ref,
                pb_ref, wo_ref, fg_ref, wup_ref, cw_ref, cb_ref, wd_ref, og_ref,
                out_ref, carry_ref, act_ref, stage_ref):
    tm = x_ref.shape[0]
    rows = FFN_SUBTILE
    subs = range(tm // rows)

    @pl.when(pl.program_id(1) == 0)
    def _():
        carry_ref[...] = jnp.zeros_like(carry_ref)

    def natural(sub, ref, g, slot):
        dil = DILATIONS[g]
        view_rows = slice(sub * rows // dil, (sub + 1) * rows // dil)
        if dil == 1:
            return ref[view_rows, :].astype(F32)
        for r in range(dil):
            for s in range(2):
                lo = r * GROUP_WIDTH + s * LANES
                stage_ref[sub, slot, s, pl.ds(r, rows // dil, stride=dil), :] = (
                    ref[view_rows, lo:lo + LANES].astype(F32))
        return jnp.concatenate([stage_ref[sub, slot, 0], stage_ref[sub, slot, 1]], axis=1)

    def merged_branches(sub):
        tok = slice(sub * rows, (sub + 1) * rows)
        l0, l1, l2 = natural(sub, l0_ref, 0, 0), natural(sub, l1_ref, 1, 0), natural(sub, l2_ref, 2, 1)
        lmax = jnp.maximum(jnp.maximum(l0, l1), l2)
        e0, e1, e2 = jnp.exp(l0 - lmax), jnp.exp(l1 - lmax), jnp.exp(l2 - lmax)
        yb = (e0 * natural(sub, o0_ref, 0, 0) + e1 * natural(sub, o1_ref, 1, 2)
              + e2 * natural(sub, o2_ref, 2, 3)) / (e0 + e1 + e2)
        merged = ta_ref[tok, :].astype(F32) + g1_ref[tok, :].astype(F32) * jnp.dot(
            yb.astype(BF16), pb_ref[...], preferred_element_type=F32)
        return merged.astype(BF16)

    def out_proj(sub, merged):
        tok = slice(sub * rows, (sub + 1) * rows)
        x1 = x_ref[tok, :] + jnp.dot(merged, wo_ref[...], preferred_element_type=F32)
        return x1, _rms(x1, fg_ref[...]).astype(BF16)

    row8 = lax.broadcasted_iota(jnp.int32, (8, FF_CHUNK), 0)

    def up_chunk(sub, c, h2, prev):
        cols = slice(c * FF_CHUNK, (c + 1) * FF_CHUNK)
        a = jnp.dot(h2, wup_ref[:, cols], preferred_element_type=F32)
        val = jnp.dot(h2, wup_ref[:, D_FF + c * FF_CHUNK:D_FF + (c + 1) * FF_CHUNK],
                      preferred_element_type=F32)
        s1 = pltpu.roll(a, 1, 0)
        s2 = pltpu.roll(a, 2, 0)
        s1 = jnp.concatenate([jnp.where(row8 < 1, pltpu.roll(prev, 1, 0), s1[:8]), s1[8:]], axis=0)
        s2 = jnp.concatenate([jnp.where(row8 < 2, pltpu.roll(prev, 2, 0), s2[:8]), s2[8:]], axis=0)
        w = cw_ref[:, cols]
        y = s2 * w[0:1] + s1 * w[1:2] + a * w[2:3] + cb_ref[:, cols]
        act_ref[sub * rows:(sub + 1) * rows, cols] = (_gelu(y) * val).astype(BF16)
        return a[rows - 8:]

    merged = [merged_branches(sub) for sub in subs]
    xh = [out_proj(sub, merged[sub]) for sub in subs]
    for c in range(N_FF_CHUNKS):
        cols = slice(c * FF_CHUNK, (c + 1) * FF_CHUNK)
        prev = carry_ref[:, cols]
        for sub in subs:
            prev = up_chunk(sub, c, xh[sub][1], prev)
        carry_ref[:, cols] = prev
    for sub in subs:
        tok = slice(sub * rows, (sub + 1) * rows)
        x2 = xh[sub][0] + jnp.dot(act_ref[tok, :], wd_ref[...], preferred_element_type=F32)
        out_ref[tok, :] = _rms(x2, og_ref[...])


def _ffn(x, ta, g1, os_, ls_, w_b, w_out, ffn_g, w_up, conv_w, conv_b, w_down, final_g, batch, seq):
    tm = FFN_TILE
    tiles = seq // tm
    row_spec = lambda w: pl.BlockSpec((tm, w), lambda b, i: (b * tiles + i, 0))
    grp_specs = [pl.BlockSpec((tm // d, d * GROUP_WIDTH), lambda b, i: (b * tiles + i, 0))
                 for d in DILATIONS]
    return pl.pallas_call(
        _ffn_kernel,
        grid=(batch, tiles),
        in_specs=[row_spec(D_MODEL), row_spec(D_MODEL), row_spec(D_MODEL)]
                 + grp_specs * 2
                 + [_resident((GROUP_WIDTH, D_MODEL)), _resident((D_MODEL, D_MODEL)),
                    _resident((1, D_MODEL)), _resident((D_MODEL, 2 * D_FF)),
                    _resident((3, D_FF)), _resident((1, D_FF)), _resident((D_FF, D_MODEL)),
                    _resident((1, D_MODEL))],
        out_specs=row_spec(D_MODEL),
        out_shape=jax.ShapeDtypeStruct((batch * seq, D_MODEL), F32),
        scratch_shapes=[pltpu.VMEM((8, D_FF), F32), pltpu.VMEM((tm, D_FF), BF16),
                        pltpu.VMEM((tm // FFN_SUBTILE, 4, 2, FFN_SUBTILE, LANES), F32)],
        compiler_params=pltpu.CompilerParams(
            dimension_semantics=("arbitrary", "arbitrary"), vmem_limit_bytes=VMEM_LIMIT_BYTES),
        name="mix_ffn",
    )(x, ta, g1, *os_, *ls_, w_b, w_out, ffn_g, w_up, conv_w, conv_b, w_down, final_g)


def _rotary_first(w):
    d = w.shape[0]
    t = w.reshape(d, 3, HEADS_PER_GROUP, HEAD_DIM)
    parts = [t[..., :ROT_HALF].reshape(d, 3, HEADS_PER_GROUP * ROT_HALF),
             t[..., ROT_HALF:ROT_DIM].reshape(d, 3, HEADS_PER_GROUP * ROT_HALF),
             t[..., ROT_DIM:].reshape(d, 3, HEADS_PER_GROUP * PASS_DIM)]
    return jnp.concatenate(parts, axis=-1).reshape(d, 3 * GROUP_WIDTH)


def kernel(x, positions, mix_norm_g, w_in, gmlp_norm_g, w_spatial, b_spatial, w_branch_a,
           w_branch_b, w_out, ffn_norm_g, w_up, conv_w, conv_b, w_down, final_norm_g):
    batch, seq, d = x.shape
    assert w_in.shape[0] == 1, "single-layer block: the final norm is fused into the FFN stage"
    layer = 0
    n = batch * seq
    inv_freq = ROPE_THETA ** (-jnp.arange(0, ROT_DIM, 2, dtype=F32) / ROT_DIM)
    inv_freq = jnp.broadcast_to(inv_freq[:, None], (ROT_HALF, LANES))
    pos2 = positions.reshape(n // PROJ_TILE, 1, PROJ_TILE)
    xf = x.reshape(n, d)
    wi = w_in[layer]
    wi = jnp.concatenate([wi[:, :OFF_Q], _rotary_first(wi[:, OFF_Q:OFF_K]),
                          _rotary_first(wi[:, OFF_K:OFF_V]), wi[:, OFF_V:]], axis=1).astype(BF16)
    bias_tbl = jnp.repeat(b_spatial[layer].T, GMLP_WIDTH // GMLP_GROUPS, axis=1)
    res = _projection(xf, pos2, mix_norm_g[layer].reshape(1, d), wi,
                      gmlp_norm_g[layer].reshape(1, GMLP_WIDTH), w_spatial[layer], bias_tbl,
                      w_branch_a[layer].astype(BF16), inv_freq)
    qs, ks, vs, ta, g1 = res[0:3], res[3:6], res[6:9], res[9], res[10]
    outs, lses = [], []
    for g, (dil, rps) in enumerate(zip(DILATIONS, RESIDUES_PER_STEP)):
        o, lse = _attention(qs[g], ks[g], vs[g], batch, seq, dil, rps)
        outs.append(o)
        lses.append(lse)
    out = _ffn(xf, ta, g1, outs, lses, w_branch_b[layer].astype(BF16), w_out[layer].astype(BF16),
               ffn_norm_g[layer].reshape(1, d), w_up[layer].astype(BF16), conv_w[layer],
               conv_b[layer].reshape(1, D_FF), w_down[layer].astype(BF16),
               final_norm_g.reshape(1, d), batch, seq)
    return out.reshape(batch, seq, d)
```

```python
import functools

import jax
import jax.numpy as jnp
import numpy as np
from jax import lax
from jax.experimental import pallas as pl
from jax.experimental.pallas import tpu as pltpu

F32 = jnp.float32
BF16 = jnp.bfloat16

D_MODEL = 1024
EPS = 1e-6
GMLP_WIDTH = 768
GMLP_GROUPS = 4
GMLP_CHUNK = 128
HEAD_DIM = 64
HEADS_PER_GROUP = 4
GROUP_WIDTH = HEADS_PER_GROUP * HEAD_DIM
DILATIONS = (1, 4, 16)
RESIDUES_PER_STEP = (1, 1, 4)
BAND = 128
ATTN_BLOCK = 128
ATTN_GROUP = 8
ROPE_THETA = 500000.0
ROT_DIM = HEAD_DIM // 4
ROT_HALF = ROT_DIM // 2
PASS_DIM = HEAD_DIM - ROT_DIM
D_FF = 2816
FF_CHUNK = 256
N_FF_CHUNKS = D_FF // FF_CHUNK

OFF_Q = 2 * GMLP_WIDTH
OFF_K = OFF_Q + 3 * GROUP_WIDTH
OFF_V = OFF_K + 3 * GROUP_WIDTH
OFF_G0 = OFF_V + 3 * GROUP_WIDTH
OFF_G1 = OFF_G0 + D_MODEL
IN_WIDTH = OFF_G1 + D_MODEL

LANES = 128
PROJ_TILE = 512
PROJ_SUBTILE = 256
FFN_TILE = 512
FFN_SUBTILE = 256
VMEM_LIMIT_BYTES = 56 * 1024 * 1024


def _erf(x):
    return lax.erf(x)


def _gelu(x):
    return 0.5 * x * (1.0 + _erf(x * np.float32(np.sqrt(0.5))))


def _sigmoid(x):
    return 1.0 / (1.0 + jnp.exp(-x))


def _rms(x, g):
    ms = jnp.mean(x * x, axis=-1, keepdims=True)
    return x * lax.rsqrt(ms + EPS) * g


def _proj_kernel(x_ref, pos_ref, g_ref, w_ref, gg_ref, ws_ref, bias_ref, pa_ref, invf_ref,
                 q0_ref, q1_ref, q2_ref, k0_ref, k1_ref, k2_ref, v0_ref, v1_ref, v2_ref,
                 ta_ref, g1_ref, stage_ref):
    tm = x_ref.shape[0]
    rows = PROJ_SUBTILE
    subs = range(tm // rows)
    q_refs, k_refs, v_refs = (q0_ref, q1_ref, q2_ref), (k0_ref, k1_ref, k2_ref), (v0_ref, v1_ref, v2_ref)
    row = lax.broadcasted_iota(jnp.int32, (GMLP_CHUNK, GMLP_CHUNK), 0)
    col = lax.broadcasted_iota(jnp.int32, (GMLP_CHUNK, GMLP_CHUNK), 1)
    wm = [jnp.where(row >= col, ws_ref[g], 0.0).astype(BF16) for g in range(GMLP_GROUPS)]
    low_half = lax.broadcasted_iota(jnp.int32, (GMLP_CHUNK, LANES), 1) < (LANES // 2)

    def proj(h, lo, hi):
        return jnp.dot(h, w_ref[:, lo:hi], preferred_element_type=F32)

    def gating_inputs(sub):
        h = _rms(x_ref[sub * rows:(sub + 1) * rows], g_ref[...]).astype(BF16)
        vn = _rms(_gelu(proj(h, GMLP_WIDTH, 2 * GMLP_WIDTH)), gg_ref[...]).astype(BF16)
        u = _gelu(proj(h, 0, GMLP_WIDTH))
        return h, u, vn

    def store_by_residue(sub, halves, ref, g, slot):
        dil = DILATIONS[g]
        out_rows = slice(sub * rows // dil, (sub + 1) * rows // dil)
        if dil == 1:
            for s, half in enumerate(halves):
                ref[out_rows, s * LANES:(s + 1) * LANES] = half.astype(BF16)
            return
        for s, half in enumerate(halves):
            stage_ref[sub, slot, s] = half
        for r in range(dil):
            for s in range(2):
                lo = r * GROUP_WIDTH + s * LANES
                ref[out_rows, lo:lo + LANES] = stage_ref[
                    sub, slot, s, pl.ds(r, rows // dil, stride=dil), :].astype(BF16)

    def gates_and_qkv(sub, h):
        g0 = _sigmoid(proj(h, OFF_G0, OFF_G1))
        g1_ref[sub * rows:(sub + 1) * rows] = _sigmoid(proj(h, OFF_G1, IN_WIDTH)).astype(BF16)
        pos = pos_ref[:, sub * rows:(sub + 1) * rows].astype(F32)
        ang = jnp.concatenate([invf_ref[...] * pos[:, b * LANES:(b + 1) * LANES]
                               for b in range(rows // LANES)], axis=1)
        n_rot = 2 * HEADS_PER_GROUP
        cos = jnp.concatenate([jnp.cos(ang)] * n_rot + [jnp.ones((LANES - 8 * n_rot, rows), F32)], axis=0).T
        sin = jnp.concatenate([jnp.sin(ang)] * n_rot + [jnp.zeros((LANES - 8 * n_rot, rows), F32)], axis=0).T
        lane = lax.broadcasted_iota(jnp.int32, (rows, LANES), 1)
        sin_a = jnp.where(lane < 32, -sin, 0.0)
        sin_b = jnp.where(lane < 32, 0.0, sin)
        for off, refs, slot0 in ((OFF_Q, q_refs, 0), (OFF_K, k_refs, 2)):
            for g, ref in enumerate(refs):
                t = proj(h, off + g * GROUP_WIDTH, off + (g + 1) * GROUP_WIDTH)
                t0 = t[:, :LANES]
                t0 = t0 * cos + pltpu.roll(t0, LANES - 32, 1) * sin_a + pltpu.roll(t0, 32, 1) * sin_b
                store_by_residue(sub, (t0, t[:, LANES:]), ref, g, slot0 + g - 1)
        for g, ref in enumerate(v_refs):
            t = proj(h, OFF_V + g * GROUP_WIDTH, OFF_V + (g + 1) * GROUP_WIDTH)
            store_by_residue(sub, (t[:, :LANES], t[:, LANES:]), ref, g, 4 + g - 1)
        return g0

    def spatial_gating(sub, u, vn, g0):
        bias = bias_ref[...]
        ya_rows = []
        for c in range(rows // GMLP_CHUNK):
            vc = vn[c * GMLP_CHUNK:(c + 1) * GMLP_CHUNK]
            p0 = jnp.dot(wm[0], vc[:, 0:256], preferred_element_type=F32)
            p1 = jnp.dot(wm[1], vc[:, 128:384], preferred_element_type=F32)
            p2 = jnp.dot(wm[2], vc[:, 384:640], preferred_element_type=F32)
            p3 = jnp.dot(wm[3], vc[:, 512:768], preferred_element_type=F32)
            mixed = jnp.concatenate(
                [p0[:, :LANES], jnp.where(low_half, p0[:, LANES:], p1[:, :LANES]), p1[:, LANES:],
                 p2[:, :LANES], jnp.where(low_half, p2[:, LANES:], p3[:, :LANES]), p3[:, LANES:]],
                axis=1) + bias
            ya_rows.append(u[c * GMLP_CHUNK:(c + 1) * GMLP_CHUNK] * mixed)
        ya = jnp.concatenate(ya_rows, axis=0).astype(BF16)
        ta_ref[sub * rows:(sub + 1) * rows] = (
            g0 * jnp.dot(ya, pa_ref[...], preferred_element_type=F32)).astype(BF16)

    chains = [gating_inputs(sub) for sub in subs]
    gates = [gates_and_qkv(sub, chains[sub][0]) for sub in subs]
    for sub in subs:
        spatial_gating(sub, chains[sub][1], chains[sub][2], gates[sub])


def _resident(shape):
    nd = len(shape)
    return pl.BlockSpec(shape, lambda *_: (0,) * nd, pipeline_mode=pl.Buffered(1))


def _projection(x2, pos2, mix_g, w_in, gmlp_g, w_s, bias_tbl, w_a, inv_freq):
    n = x2.shape[0]
    tm = PROJ_TILE
    row_spec = lambda w: pl.BlockSpec((tm, w), lambda i: (i, 0))
    grp_specs = [pl.BlockSpec((tm // d, d * GROUP_WIDTH), lambda i: (i, 0)) for d in DILATIONS] * 3
    grp_shapes = [jax.ShapeDtypeStruct((n // d, d * GROUP_WIDTH), BF16) for d in DILATIONS] * 3
    wide = jax.ShapeDtypeStruct((n, D_MODEL), BF16)
    return pl.pallas_call(
        _proj_kernel,
        grid=(n // tm,),
        in_specs=[row_spec(D_MODEL), pl.BlockSpec((None, 1, tm), lambda i: (i, 0, 0)),
                  _resident((1, D_MODEL)),
                  _resident((D_MODEL, IN_WIDTH)), _resident((1, GMLP_WIDTH)),
                  _resident((GMLP_GROUPS, GMLP_CHUNK, GMLP_CHUNK)),
                  _resident((GMLP_CHUNK, GMLP_WIDTH)), _resident((GMLP_WIDTH, D_MODEL)),
                  _resident((ROT_HALF, LANES))],
        out_specs=grp_specs + [row_spec(D_MODEL)] * 2,
        out_shape=grp_shapes + [wide] * 2,
        scratch_shapes=[pltpu.VMEM((tm // PROJ_SUBTILE, 6, 2, PROJ_SUBTILE, LANES), F32)],
        compiler_params=pltpu.CompilerParams(
            dimension_semantics=("arbitrary",), vmem_limit_bytes=VMEM_LIMIT_BYTES),
        name="projection",
    )(x2, pos2, mix_g, w_in, gmlp_g, w_s, bias_tbl, w_a, inv_freq)


def _head_masks():
    lane = lax.broadcasted_iota(jnp.int32, (1, GROUP_WIDTH), 1)
    qk_head = jnp.where(lane < 2 * 32, (lane % 32) // ROT_HALF, (lane - 2 * 32) // PASS_DIM)
    v_head = lane // HEAD_DIM
    return qk_head, v_head


def _attn_q_masks():
    qk_head, _ = _head_masks()
    heads = jnp.arange(HEADS_PER_GROUP, dtype=jnp.int32)[:, None, None]
    mask = jnp.where(qk_head[None] == heads, 1.0, 0.0)
    return jnp.broadcast_to(mask, (HEADS_PER_GROUP, ATTN_BLOCK, GROUP_WIDTH)).astype(BF16)


def _attn_kernel(q_ref, k_ref, v_ref, qm_ref, o_ref, lse_ref, *, n_res, n_blocks, group):
    nh = HEADS_PER_GROUP
    blk = ATTN_BLOCK
    _, v_head = _head_masks()
    qi = lax.broadcasted_iota(jnp.int32, (blk, 2 * blk), 0)
    ki = lax.broadcasted_iota(jnp.int32, (blk, 2 * blk), 1)
    band2 = jnp.where((ki >= qi) & (ki <= qi + BAND), 0.0, -jnp.inf).astype(F32)
    band1 = band2[:, blk:]
    band2 = jnp.concatenate([band2] * nh, axis=0)
    band1 = jnp.concatenate([band1] * nh, axis=0)
    low_half = lax.broadcasted_iota(jnp.int32, (blk, LANES), 1) < HEAD_DIM

    def per_head(x):
        halves = [jnp.where(low_half, x[2 * c * blk:(2 * c + 1) * blk], x[(2 * c + 1) * blk:(2 * c + 2) * blk])
                  for c in range(nh // 2)]
        return jnp.concatenate(halves, axis=1)

    def scores(cols, q_rows, kv_rows, band):
        qb = q_ref[q_rows, cols]
        qs = jnp.concatenate([qb * qm_ref[h] for h in range(nh)], axis=0)
        return lax.dot_general(qs, k_ref[kv_rows, cols], (((1,), (1,)), ((), ())),
                               preferred_element_type=F32) + band

    def softmax(s):
        m = jnp.max(s, axis=-1, keepdims=True)
        p = jnp.exp2(s - m)
        return p.astype(BF16), m, jnp.sum(p, axis=-1, keepdims=True)

    def finish(cols, q_rows, kv_rows, p, m, den):
        pv = jnp.dot(p, v_ref[kv_rows, cols], preferred_element_type=F32)
        out = pv[(nh - 1) * blk:]
        for h in range(nh - 2, -1, -1):
            out = jnp.where(v_head == h, pv[h * blk:(h + 1) * blk], out)
        den = per_head(jnp.broadcast_to(den, (nh * blk, LANES)))
        o_ref[q_rows, cols] = (out / den).astype(o_ref.dtype)
        m_wide = per_head(jnp.broadcast_to(m, (nh * blk, LANES)))
        lse_ref[q_rows, cols] = (m_wide + jnp.log2(den)) * np.float32(np.log(2.0))

    def block_group(cols, blocks):
        s = [scores(cols, *b) for b in blocks]
        pmd = [softmax(x) for x in s]
        for b, (p, m, den) in zip(blocks, pmd):
            finish(cols, b[0], b[1], p, m, den)

    def later_block(q0):
        q0 = pl.multiple_of(q0, blk)
        return pl.ds(q0, blk), pl.ds(q0 - blk, 2 * blk), band2

    for r in range(n_res):
        cols = slice(r * GROUP_WIDTH, (r + 1) * GROUP_WIDTH)
        first = [(slice(0, blk), slice(0, blk), band1)]
        first += [(slice(j * blk, (j + 1) * blk), slice((j - 1) * blk, (j + 1) * blk), band2)
                  for j in range(1, group)]
        block_group(cols, first)

        def body(i, carry, cols=cols):
            base = pl.multiple_of(i * (group * blk), group * blk)
            block_group(cols, [later_block(base + j * blk) for j in range(group)])
            return carry

        lax.fori_loop(1, n_blocks // group, body, 0)


def _attention(q, k, v, batch, seq, dil, res_per_step):
    sub = seq // dil
    width = res_per_step * GROUP_WIDTH
    view = lambda t: t.reshape(batch, sub, dil * GROUP_WIDTH)
    spec = pl.BlockSpec((None, sub, width), lambda b, r: (b, 0, r))
    n_blocks = sub // ATTN_BLOCK
    qm = _attn_q_masks()
    kern =functools.partial(_attn_kernel, n_res=res_per_step, n_blocks=n_blocks,
                             group=min(ATTN_GROUP, n_blocks))
    o, lse = pl.pallas_call(
        kern,
        grid=(batch, dil // res_per_step),
        in_specs=[spec, spec, spec, _resident(qm.shape)],
        out_specs=[spec, spec],
        out_shape=[jax.ShapeDtypeStruct((batch, sub, dil * GROUP_WIDTH), BF16),
                   jax.ShapeDtypeStruct((batch, sub, dil * GROUP_WIDTH), F32)],
        compiler_params=pltpu.CompilerParams(
            dimension_semantics=("arbitrary", "arbitrary"), vmem_limit_bytes=VMEM_LIMIT_BYTES),
        name=f"attention_d{dil}",
    )(view(q), view(k), view(v), qm)
    return o.reshape(batch * sub, dil * GROUP_WIDTH), lse.reshape(batch * sub, dil * GROUP_WIDTH)


def _ffn_kernel(x_ref, ta_ref, g1_ref, o0_ref, o1_ref, o2_ref, l0_ref, l1_ref, l2_ref,
                pb_ref, wo_ref, fg_ref, wup_ref, cw_ref, cb_ref, wd_ref, og_ref,
                out_ref, carry_ref, act_ref, stage_ref):
    tm = x_ref.shape[0]
    rows = FFN_SUBTILE
    subs = range(tm // rows)

    @pl.when(pl.program_id(1) == 0)
    def _():
        carry_ref[...] = jnp.zeros_like(carry_ref)

    def natural(sub, ref, g, slot):
        dil = DILATIONS[g]
        view_rows = slice(sub * rows // dil, (sub + 1) * rows // dil)
        if dil == 1:
            return ref[view_rows, :].astype(F32)
        for r in range(dil):
            for s in range(2):
                lo = r * GROUP_WIDTH + s * LANES
                stage_ref[sub, slot, s, pl.ds(r, rows // dil, stride=dil), :] = (
                    ref[view_rows, lo:lo + LANES].astype(F32))
        return jnp.concatenate([stage_ref[sub, slot, 0], stage_ref[sub, slot, 1]], axis=1)

    def merged_branches(sub):
        tok = slice(sub * rows, (sub + 1) * rows)
        l0, l1, l2 = natural(sub, l0_ref, 0, 0), natural(sub, l1_ref, 1, 0), natural(sub, l2_ref, 2, 1)
        lmax = jnp.maximum(jnp.maximum(l0, l1), l2)
        e0, e1, e2 = jnp.exp(l0 - lmax), jnp.exp(l1 - lmax), jnp.exp(l2 - lmax)
        yb = (e0 * natural(sub, o0_ref, 0, 0) + e1 * natural(sub, o1_ref, 1, 2)
              + e2 * natural(sub, o2_ref, 2, 3)) / (e0 + e1 + e2)
        merged = ta_ref[tok, :].astype(F32) + g1_ref[tok, :].astype(F32) * jnp.dot(
            yb.astype(BF16), pb_ref[...], preferred_element_type=F32)
        return merged.astype(BF16)

    def out_proj(sub, merged):
        tok = slice(sub * rows, (sub + 1) * rows)
        x1 = x_ref[tok, :] + jnp.dot(merged, wo_ref[...], preferred_element_type=F32)
        return x1, _rms(x1, fg_ref[...]).astype(BF16)

    row8 = lax.broadcasted_iota(jnp.int32, (8, FF_CHUNK), 0)

    def up_chunk(sub, c, h2, prev):
        cols = slice(c * FF_CHUNK, (c + 1) * FF_CHUNK)
        a = jnp.dot(h2, wup_ref[:, cols], preferred_element_type=F32)
        val = jnp.dot(h2, wup_ref[:, D_FF + c * FF_CHUNK:D_FF + (c + 1) * FF_CHUNK],
                      preferred_element_type=F32)
        s1 = pltpu.roll(a, 1, 0)
        s2 = pltpu.roll(a, 2, 0)
        s1 = jnp.concatenate([jnp.where(row8 < 1, pltpu.roll(prev, 1, 0), s1[:8]), s1[8:]], axis=0)
        s2 = jnp.concatenate([jnp.where(row8 < 2, pltpu.roll(prev, 2, 0), s2[:8]), s2[8:]], axis=0)
        w = cw_ref[:, cols]
        y = s2 * w[0:1] + s1 * w[1:2] + a * w[2:3] + cb_ref[:, cols]
        act_ref[sub * rows:(sub + 1) * rows, cols] = (_gelu(y) * val).astype(BF16)
        return a[rows - 8:]

    merged = [merged_branches(sub) for sub in subs]
    xh = [out_proj(sub, merged[sub]) for sub in subs]
    for c in range(N_FF_CHUNKS):
        cols = slice(c * FF_CHUNK, (c + 1) * FF_CHUNK)
        prev = carry_ref[:, cols]
        for sub in subs:
            prev = up_chunk(sub, c, xh[sub][1], prev)
        carry_ref[:, cols] = prev
    for sub in subs:
        tok = slice(sub * rows, (sub + 1) * rows)
        x2 = xh[sub][0] + jnp.dot(act_ref[tok, :], wd_ref[...], preferred_element_type=F32)
        out_ref[tok, :] = _rms(x2, og_ref[...])


def _ffn(x, ta, g1, os_, ls_, w_b, w_out, ffn_g, w_up, conv_w, conv_b, w_down, final_g, batch, seq):
    tm = FFN_TILE
    tiles = seq // tm
    row_spec = lambda w: pl.BlockSpec((tm, w), lambda b, i: (b * tiles + i, 0))
    grp_specs = [pl.BlockSpec((tm // d, d * GROUP_WIDTH), lambda b, i: (b * tiles + i, 0))
                 for d in DILATIONS]
    return pl.pallas_call(
        _ffn_kernel,
        grid=(batch, tiles),
        in_specs=[row_spec(D_MODEL), row_spec(D_MODEL), row_spec(D_MODEL)]
                 + grp_specs * 2
                 + [_resident((GROUP_WIDTH, D_MODEL)), _resident((D_MODEL, D_MODEL)),
                    _resident((1, D_MODEL)), _resident((D_MODEL, 2 * D_FF)),
                    _resident((3, D_FF)), _resident((1, D_FF)), _resident((D_FF, D_MODEL)),
                    _resident((1, D_MODEL))],
        out_specs=row_spec(D_MODEL),
        out_shape=jax.ShapeDtypeStruct((batch * seq, D_MODEL), F32),
        scratch_shapes=[pltpu.VMEM((8, D_FF), F32), pltpu.VMEM((tm, D_FF), BF16),
                        pltpu.VMEM((tm // FFN_SUBTILE, 4, 2, FFN_SUBTILE, LANES), F32)],
        compiler_params=pltpu.CompilerParams(
            dimension_semantics=("arbitrary", "arbitrary"), vmem_limit_bytes=VMEM_LIMIT_BYTES),
        name="mix_ffn",
    )(x, ta, g1, *os_, *ls_, w_b, w_out, ffn_g, w_up, conv_w, conv_b, w_down, final_g)


def _rotary_first(w):
    d = w.shape[0]
    t = w.reshape(d, 3, HEADS_PER_GROUP, HEAD_DIM)
    parts = [t[..., :ROT_HALF].reshape(d, 3, HEADS_PER_GROUP * ROT_HALF),
             t[..., ROT_HALF:ROT_DIM].reshape(d, 3, HEADS_PER_GROUP * ROT_HALF),
             t[..., ROT_DIM:].reshape(d, 3, HEADS_PER_GROUP * PASS_DIM)]
    return jnp.concatenate(parts, axis=-1).reshape(d, 3 * GROUP_WIDTH)


def kernel(x, positions, mix_norm_g, w_in, gmlp_norm_g, w_spatial, b_spatial, w_branch_a,
           w_branch_b, w_out, ffn_norm_g, w_up, conv_w, conv_b, w_down, final_norm_g):
    batch, seq, d = x.shape
    assert w_in.shape[0] == 1, "single-layer block: the final norm is fused into the FFN stage"
    layer = 0
    n = batch * seq
    inv_freq = ROPE_THETA ** (-jnp.arange(0, ROT_DIM, 2, dtype=F32) / ROT_DIM)
    inv_freq = jnp.broadcast_to(inv_freq[:, None], (ROT_HALF, LANES))
    pos2 = positions.reshape(n // PROJ_TILE, 1, PROJ_TILE)
    xf = x.reshape(n, d)
    wi = w_in[layer]
    wi = jnp.concatenate([wi[:, :OFF_Q], _rotary_first(wi[:, OFF_Q:OFF_K]) * HEAD_DIM ** -0.5,
                          _rotary_first(wi[:, OFF_K:OFF_V]) * np.float32(np.log2(np.e)),
                          wi[:, OFF_V:]], axis=1).astype(BF16)
    bias_tbl = jnp.repeat(b_spatial[layer].T, GMLP_WIDTH // GMLP_GROUPS, axis=1)
    res = _projection(xf, pos2, mix_norm_g[layer].reshape(1, d), wi,
                      gmlp_norm_g[layer].reshape(1, GMLP_WIDTH), w_spatial[layer], bias_tbl,
                      w_branch_a[layer].astype(BF16), inv_freq)
    qs, ks, vs, ta, g1 = res[0:3], res[3:6], res[6:9], res[9], res[10]
    outs, lses = [], []
    for g, (dil, rps) in enumerate(zip(DILATIONS, RESIDUES_PER_STEP)):
        o, lse = _attention(qs[g], ks[g], vs[g], batch, seq, dil, rps)
        outs.append(o)
        lses.append(lse)
    out = _ffn(xf, ta, g1, outs, lses, w_branch_b[layer].astype(BF16), w_out[layer].astype(BF16),
               ffn_norm_g[layer].reshape(1, d), w_up[layer].astype(BF16), conv_w[layer],
               conv_b[layer].reshape(1, D_FF), w_down[layer].astype(BF16),
               final_norm_g.reshape(1, d), batch, seq)
    return out.reshape(batch, seq, d)
```

```python
import functools

import jax
import jax.numpy as jnp
import numpy as np
from jax import lax
from jax.experimental import pallas as pl
from jax.experimental.pallas import tpu as pltpu

F32 = jnp.float32
BF16 = jnp.bfloat16

D_MODEL = 1024
EPS = 1e-6
GMLP_WIDTH = 768
GMLP_GROUPS = 4
GMLP_CHUNK = 128
HEAD_DIM = 64
HEADS_PER_GROUP = 4
GROUP_WIDTH = HEADS_PER_GROUP * HEAD_DIM
DILATIONS = (1, 4, 16)
RESIDUES_PER_STEP = (1, 4, 16)
BAND = 128
ATTN_BLOCK = 128
ATTN_GROUP = 8
ROPE_THETA = 500000.0
ROT_DIM = HEAD_DIM // 4
ROT_HALF = ROT_DIM // 2
PASS_DIM = HEAD_DIM - ROT_DIM
D_FF = 2816
FF_CHUNK = 256
N_FF_CHUNKS = D_FF // FF_CHUNK

OFF_Q = 2 * GMLP_WIDTH
OFF_K = OFF_Q + 3 * GROUP_WIDTH
OFF_V = OFF_K + 3 * GROUP_WIDTH
OFF_G0 = OFF_V + 3 * GROUP_WIDTH
OFF_G1 = OFF_G0 + D_MODEL
IN_WIDTH = OFF_G1 + D_MODEL

LANES = 128
PROJ_TILE = 512
PROJ_SUBTILE = 256
FFN_TILE = 512
FFN_SUBTILE = 256
VMEM_LIMIT_BYTES = 56 * 1024 * 1024


def _erf(x):
    return lax.erf(x)


def _gelu(x):
    return 0.5 * x * (1.0 + _erf(x * np.float32(np.sqrt(0.5))))


def _sigmoid(x):
    return 1.0 / (1.0 + jnp.exp(-x))


def _rms(x, g):
    ms = jnp.mean(x * x, axis=-1, keepdims=True)
    return x * lax.rsqrt(ms + EPS) * g


def _proj_kernel(x_ref, pos_ref, g_ref, w_ref, gg_ref, ws_ref, bias_ref, pa_ref, invf_ref,
                 q0_ref, q1_ref, q2_ref, k0_ref, k1_ref, k2_ref, v0_ref, v1_ref, v2_ref,
                 ta_ref, g1_ref, stage_ref):
    tm = x_ref.shape[0]
    rows = PROJ_SUBTILE
    subs = range(tm // rows)
    q_refs, k_refs, v_refs = (q0_ref, q1_ref, q2_ref), (k0_ref, k1_ref, k2_ref), (v0_ref, v1_ref, v2_ref)
    row = lax.broadcasted_iota(jnp.int32, (GMLP_CHUNK, GMLP_CHUNK), 0)
    col = lax.broadcasted_iota(jnp.int32, (GMLP_CHUNK, GMLP_CHUNK), 1)
    wm = [jnp.where(row >= col, ws_ref[g], 0.0).astype(BF16) for g in range(GMLP_GROUPS)]
    low_half = lax.broadcasted_iota(jnp.int32, (GMLP_CHUNK, LANES), 1) < (LANES // 2)

    def proj(h, lo, hi):
        return jnp.dot(h, w_ref[:, lo:hi], preferred_element_type=F32)

    def gating_inputs(sub):
        h = _rms(x_ref[sub * rows:(sub + 1) * rows], g_ref[...]).astype(BF16)
        vn = _rms(_gelu(proj(h, GMLP_WIDTH, 2 * GMLP_WIDTH)), gg_ref[...]).astype(BF16)
        u = _gelu(proj(h, 0, GMLP_WIDTH))
        return h, u, vn

    def store_by_residue(sub, halves, ref, g, slot):
        dil = DILATIONS[g]
        out_rows = slice(sub * rows // dil, (sub + 1) * rows // dil)
        if dil == 1:
            for s, half in enumerate(halves):
                ref[out_rows, s * LANES:(s + 1) * LANES] = half.astype(BF16)
            return
        for s, half in enumerate(halves):
            stage_ref[sub, slot, s] = half
        for r in range(dil):
            for s in range(2):
                lo = r * GROUP_WIDTH + s * LANES
                ref[out_rows, lo:lo + LANES] = stage_ref[
                    sub, slot, s, pl.ds(r, rows // dil, stride=dil), :].astype(BF16)

    def gates_and_qkv(sub, h):
        g0 = _sigmoid(proj(h, OFF_G0, OFF_G1))
        g1_ref[sub * rows:(sub + 1) * rows] = _sigmoid(proj(h, OFF_G1, IN_WIDTH)).astype(BF16)
        pos = pos_ref[:, sub * rows:(sub + 1) * rows].astype(F32)
        ang = jnp.concatenate([invf_ref[...] * pos[:, b * LANES:(b + 1) * LANES]
                               for b in range(rows // LANES)], axis=1)
        n_rot = 2 * HEADS_PER_GROUP
        cos = jnp.concatenate([jnp.cos(ang)] * n_rot + [jnp.ones((LANES - 8 * n_rot, rows), F32)], axis=0).T
        sin = jnp.concatenate([jnp.sin(ang)] * n_rot + [jnp.zeros((LANES - 8 * n_rot, rows), F32)], axis=0).T
        lane = lax.broadcasted_iota(jnp.int32, (rows, LANES), 1)
        sin_a = jnp.where(lane < 32, -sin, 0.0)
        sin_b = jnp.where(lane < 32, 0.0, sin)
        for off, refs, slot0 in ((OFF_Q, q_refs, 0), (OFF_K, k_refs, 2)):
            for g, ref in enumerate(refs):
                t = proj(h, off + g * GROUP_WIDTH, off + (g + 1) * GROUP_WIDTH)
                t0 = t[:, :LANES]
                t0 = t0 * cos + pltpu.roll(t0, LANES - 32, 1) * sin_a + pltpu.roll(t0, 32, 1) * sin_b
                store_by_residue(sub, (t0, t[:, LANES:]), ref, g, slot0 + g - 1)
        for g, ref in enumerate(v_refs):
            t = proj(h, OFF_V + g * GROUP_WIDTH, OFF_V + (g + 1) * GROUP_WIDTH)
            store_by_residue(sub, (t[:, :LANES], t[:, LANES:]), ref, g, 4 + g - 1)
        return g0

    def spatial_gating(sub, u, vn, g0):
        bias = bias_ref[...]
        ya_rows = []
        for c in range(rows // GMLP_CHUNK):
            vc = vn[c * GMLP_CHUNK:(c + 1) * GMLP_CHUNK]
            p0 = jnp.dot(wm[0], vc[:, 0:256], preferred_element_type=F32)
            p1 = jnp.dot(wm[1], vc[:, 128:384], preferred_element_type=F32)
            p2 = jnp.dot(wm[2], vc[:, 384:640], preferred_element_type=F32)
            p3 = jnp.dot(wm[3], vc[:, 512:768], preferred_element_type=F32)
            mixed = jnp.concatenate(
                [p0[:, :LANES], jnp.where(low_half, p0[:, LANES:], p1[:, :LANES]), p1[:, LANES:],
                 p2[:, :LANES], jnp.where(low_half, p2[:, LANES:], p3[:, :LANES]), p3[:, LANES:]],
                axis=1) + bias
            ya_rows.append(u[c * GMLP_CHUNK:(c + 1) * GMLP_CHUNK] * mixed)
        ya = jnp.concatenate(ya_rows, axis=0).astype(BF16)
        ta_ref[sub * rows:(sub + 1) * rows] = (
            g0 * jnp.dot(ya, pa_ref[...], preferred_element_type=F32)).astype(BF16)

    chains = [gating_inputs(sub) for sub in subs]
    gates = [gates_and_qkv(sub, chains[sub][0]) for sub in subs]
    for sub in subs:
        spatial_gating(sub, chains[sub][1], chains[sub][2], gates[sub])


def _resident(shape):
    nd = len(shape)
    return pl.BlockSpec(shape, lambda *_: (0,) * nd, pipeline_mode=pl.Buffered(1))


def _projection(x2, pos2, mix_g, w_in, gmlp_g, w_s, bias_tbl, w_a, inv_freq):
    n = x2.shape[0]
    tm = PROJ_TILE
    row_spec = lambda w: pl.BlockSpec((tm, w), lambda i: (i, 0))
    grp_specs = [pl.BlockSpec((tm // d, d * GROUP_WIDTH), lambda i: (i, 0)) for d in DILATIONS] * 3
    grp_shapes = [jax.ShapeDtypeStruct((n // d, d * GROUP_WIDTH), BF16) for d in DILATIONS] * 3
    wide = jax.ShapeDtypeStruct((n, D_MODEL), BF16)
    return pl.pallas_call(
        _proj_kernel,
        grid=(n // tm,),
        in_specs=[row_spec(D_MODEL), pl.BlockSpec((None, 1, tm), lambda i: (i, 0, 0)),
                  _resident((1, D_MODEL)),
                  _resident((D_MODEL, IN_WIDTH)), _resident((1, GMLP_WIDTH)),
                  _resident((GMLP_GROUPS, GMLP_CHUNK, GMLP_CHUNK)),
                  _resident((GMLP_CHUNK, GMLP_WIDTH)), _resident((GMLP_WIDTH, D_MODEL)),
                  _resident((ROT_HALF, LANES))],
        out_specs=grp_specs + [row_spec(D_MODEL)] * 2,
        out_shape=grp_shapes + [wide] * 2,
        scratch_shapes=[pltpu.VMEM((tm // PROJ_SUBTILE, 6, 2, PROJ_SUBTILE, LANES), F32)],
        compiler_params=pltpu.CompilerParams(
            dimension_semantics=("arbitrary",), vmem_limit_bytes=VMEM_LIMIT_BYTES),
        name="projection",
    )(x2, pos2, mix_g, w_in, gmlp_g, w_s, bias_tbl, w_a, inv_freq)


def _head_masks():
    lane = lax.broadcasted_iota(jnp.int32, (1, GROUP_WIDTH), 1)
    qk_head = jnp.where(lane < 2 * 32, (lane % 32) // ROT_HALF, (lane - 2 * 32) // PASS_DIM)
    v_head = lane // HEAD_DIM
    return qk_head, v_head


def _attn_q_masks():
    qk_head, _ = _head_masks()
    heads = jnp.arange(HEADS_PER_GROUP, dtype=jnp.int32)[:, None, None]
    mask = jnp.where(qk_head[None] == heads, 1.0, 0.0)
    return jnp.broadcast_to(mask, (HEADS_PER_GROUP, ATTN_BLOCK, GROUP_WIDTH)).astype(BF16)


def _attn_kernel(q_ref, k_ref, v_ref, qm_ref, o_ref, lse_ref, *, n_res, n_blocks, group):
    nh = HEADS_PER_GROUP
    blk = ATTN_BLOCK
    _, v_head = _head_masks()
    qi = lax.broadcasted_iota(jnp.int32, (blk, 2 * blk), 0)
    ki = lax.broadcasted_iota(jnp.int32, (blk, 2 * blk), 1)
    band2 = jnp.where((ki >= qi) & (ki <= qi + BAND), 0.0, -jnp.inf).astype(F32)
    band1 = band2[:, blk:]
    band2 = jnp.concatenate([band2] * nh, axis=0)
    band1 = jnp.concatenate([band1] * nh, axis=0)
    low_half = lax.broadcasted_iota(jnp.int32, (blk, LANES), 1) < HEAD_DIM

    def per_head(x):
        halves = [jnp.where(low_half, x[2 * c * blk:(2 * c + 1) * blk], x[(2 * c + 1) * blk:(2 * c + 2) * blk])
                  for c in range(nh // 2)]
        return jnp.concatenate(halves, axis=1)

    def scores(cols, q_rows, kv_rows, band):
        qb = q_ref[q_rows, cols]
        qs = jnp.concatenate([qb * qm_ref[h] for h in range(nh)], axis=0)
        return lax.dot_general(qs, k_ref[kv_rows, cols], (((1,), (1,)), ((), ())),
                               preferred_element_type=F32) + band

    def softmax(s):
        m = jnp.max(s, axis=-1, keepdims=True)
        p = jnp.exp2(s - m)
        return p.astype(BF16), m, jnp.sum(p, axis=-1, keepdims=True)

    def finish(cols, q_rows, kv_rows, p, m, den):
        pv = jnp.dot(p, v_ref[kv_rows, cols], preferred_element_type=F32)
        out = pv[(nh - 1) * blk:]
        for h in range(nh - 2, -1, -1):
            out = jnp.where(v_head == h, pv[h * blk:(h + 1) * blk], out)
        den = per_head(jnp.broadcast_to(den, (nh * blk, LANES)))
        o_ref[q_rows, cols] = (out / den).astype(o_ref.dtype)
        m_wide = per_head(jnp.broadcast_to(m, (nh * blk, LANES)))
        lse_ref[q_rows, cols] = (m_wide + jnp.log2(den)) * np.float32(np.log(2.0))

    def block_group(cols, blocks):
        s = [scores(cols, *b) for b in blocks]
        pmd = [softmax(x) for x in s]
        for b, (p, m, den) in zip(blocks, pmd):
            finish(cols, b[0], b[1], p, m, den)

    def later_block(q0):
        return slice(q0, q0 + blk), slice(q0 - blk, q0 + blk), band2

    for r in range(n_res):
        cols = slice(r * GROUP_WIDTH, (r + 1) * GROUP_WIDTH)
        first = [(slice(0, blk), slice(0, blk), band1)]
        block_group(cols, first + [later_block(j * blk) for j in range(1, group)])
        for i in range(1, n_blocks // group):
            block_group(cols, [later_block(j * blk) for j in range(i * group, (i + 1) * group)])


def _attention(q, k, v, batch, seq, dil, res_per_step):
    sub = seq // dil
    width = res_per_step * GROUP_WIDTH
    view = lambda t: t.reshape(batch, sub, dil * GROUP_WIDTH)
    spec = pl.BlockSpec((None, sub, width), lambda b, r: (b, 0, r))
    n_blocks = sub // ATTN_BLOCK
    qm = _attn_q_masks()
    kern =functools.partial(_attn_kernel, n_res=res_per_step, n_blocks=n_blocks,
                             group=min(ATTN_GROUP, n_blocks))
    o, lse = pl.pallas_call(
        kern,
        grid=(batch, dil // res_per_step),
        in_specs=[spec, spec, spec, _resident(qm.shape)],
        out_specs=[spec, spec],
        out_shape=[jax.ShapeDtypeStruct((batch, sub, dil * GROUP_WIDTH), BF16),
                   jax.ShapeDtypeStruct((batch, sub, dil * GROUP_WIDTH), F32)],
        compiler_params=pltpu.CompilerParams(
            dimension_semantics=("arbitrary", "arbitrary"), vmem_limit_bytes=VMEM_LIMIT_BYTES),
        name=f"attention_d{dil}",
    )(view(q), view(k), view(v), qm)
    return o.reshape(batch * sub, dil * GROUP_WIDTH), lse.reshape(batch * sub, dil * GROUP_WIDTH)


def _ffn_kernel(x_ref, ta_ref, g1_ref, o0_ref, o1_ref, o2_ref, l0_ref, l1_ref, l2_ref,
                pb_ref, wo_ref, fg_ref, wup_ref, cw_ref, cb_ref, wd_ref, og_ref,
                out_ref, carry_ref, act_ref, stage_ref):
    tm = x_ref.shape[0]
    rows = FFN_SUBTILE
    subs = range(tm // rows)

    @pl.when(pl.program_id(1) == 0)
    def _():
        carry_ref[...] = jnp.zeros_like(carry_ref)

    def natural(sub, ref, g, slot):
        dil = DILATIONS[g]
        view_rows = slice(sub * rows // dil, (sub + 1) * rows // dil)
        if dil == 1:
            return ref[view_rows, :].astype(F32)
        for r in range(dil):
            for s in range(2):
                lo = r * GROUP_WIDTH + s * LANES
                stage_ref[sub, slot, s, pl.ds(r, rows // dil, stride=dil), :] = (
                    ref[view_rows, lo:lo + LANES].astype(F32))
        return jnp.concatenate([stage_ref[sub, slot, 0], stage_ref[sub, slot, 1]], axis=1)

    def merged_branches(sub):
        tok = slice(sub * rows, (sub + 1) * rows)
        l0, l1, l2 = natural(sub, l0_ref, 0, 0), natural(sub, l1_ref, 1, 0), natural(sub, l2_ref, 2, 1)
        lmax = jnp.maximum(jnp.maximum(l0, l1), l2)
        e0, e1, e2 = jnp.exp(l0 - lmax), jnp.exp(l1 - lmax), jnp.exp(l2 - lmax)
        yb = (e0 * natural(sub, o0_ref, 0, 0) + e1 * natural(sub, o1_ref, 1, 2)
              + e2 * natural(sub, o2_ref, 2, 3)) / (e0 + e1 + e2)
        merged = ta_ref[tok, :].astype(F32) + g1_ref[tok, :].astype(F32) * jnp.dot(
            yb.astype(BF16), pb_ref[...], preferred_element_type=F32)
        return merged.astype(BF16)

    def out_proj(sub, merged):
        tok = slice(sub * rows, (sub + 1) * rows)
        x1 = x_ref[tok, :] + jnp.dot(merged, wo_ref[...], preferred_element_type=F32)
        return x1, _rms(x1, fg_ref[...]).astype(BF16)

    row8 = lax.broadcasted_iota(jnp.int32, (8, FF_CHUNK), 0)

    def up_chunk(sub, c, h2, prev):
        cols = slice(c * FF_CHUNK, (c + 1) * FF_CHUNK)
        a = jnp.dot(h2, wup_ref[:, cols], preferred_element_type=F32)
        val = jnp.dot(h2, wup_ref[:, D_FF + c * FF_CHUNK:D_FF + (c + 1) * FF_CHUNK],
                      preferred_element_type=F32)
        s1 = pltpu.roll(a, 1, 0)
        s2 = pltpu.roll(a, 2, 0)
        s1 = jnp.concatenate([jnp.where(row8 < 1, pltpu.roll(prev, 1, 0), s1[:8]), s1[8:]], axis=0)
        s2 = jnp.concatenate([jnp.where(row8 < 2, pltpu.roll(prev, 2, 0), s2[:8]), s2[8:]], axis=0)
        w = cw_ref[:, cols]
        y = s2 * w[0:1] + s1 * w[1:2] + a * w[2:3] + cb_ref[:, cols]
        act_ref[sub * rows:(sub + 1) * rows, cols] = (_gelu(y) * val).astype(BF16)
        return a[rows - 8:]

    merged = [merged_branches(sub) for sub in subs]
    xh = [out_proj(sub, merged[sub]) for sub in subs]
    for c in range(N_FF_CHUNKS):
        cols = slice(c * FF_CHUNK, (c + 1) * FF_CHUNK)
        prev = carry_ref[:, cols]
        for sub in subs:
            prev = up_chunk(sub, c, xh[sub][1], prev)
        carry_ref[:, cols] = prev
    for sub in subs:
        tok = slice(sub * rows, (sub + 1) * rows)
        x2 = xh[sub][0] + jnp.dot(act_ref[tok, :], wd_ref[...], preferred_element_type=F32)
        out_ref[tok, :] = _rms(x2, og_ref[...])


def _ffn(x, ta, g1, os_, ls_, w_b, w_out, ffn_g, w_up, conv_w, conv_b, w_down, final_g, batch, seq):
    tm = FFN_TILE
    tiles = seq // tm
    row_spec = lambda w: pl.BlockSpec((tm, w), lambda b, i: (b * tiles + i, 0))
    grp_specs = [pl.BlockSpec((tm // d, d * GROUP_WIDTH), lambda b, i: (b * tiles + i, 0))
                 for d in DILATIONS]
    return pl.pallas_call(
        _ffn_kernel,
        grid=(batch, tiles),
        in_specs=[row_spec(D_MODEL), row_spec(D_MODEL), row_spec(D_MODEL)]
                 + grp_specs * 2
                 + [_resident((GROUP_WIDTH, D_MODEL)), _resident((D_MODEL, D_MODEL)),
                    _resident((1, D_MODEL)), _resident((D_MODEL, 2 * D_FF)),
                    _resident((3, D_FF)), _resident((1, D_FF)), _resident((D_FF, D_MODEL)),
                    _resident((1, D_MODEL))],
        out_specs=row_spec(D_MODEL),
        out_shape=jax.ShapeDtypeStruct((batch * seq, D_MODEL), F32),
        scratch_shapes=[pltpu.VMEM((8, D_FF), F32), pltpu.VMEM((tm, D_FF), BF16),
                        pltpu.VMEM((tm // FFN_SUBTILE, 4, 2, FFN_SUBTILE, LANES), F32)],
        compiler_params=pltpu.CompilerParams(
            dimension_semantics=("arbitrary", "arbitrary"), vmem_limit_bytes=VMEM_LIMIT_BYTES),
        name="mix_ffn",
    )(x, ta, g1, *os_, *ls_, w_b, w_out, ffn_g, w_up, conv_w, conv_b, w_down, final_g)


def _rotary_first(w):
    d = w.shape[0]
    t = w.reshape(d, 3, HEADS_PER_GROUP, HEAD_DIM)
    parts = [t[..., :ROT_HALF].reshape(d, 3, HEADS_PER_GROUP * ROT_HALF),
             t[..., ROT_HALF:ROT_DIM].reshape(d, 3, HEADS_PER_GROUP * ROT_HALF),
             t[..., ROT_DIM:].reshape(d, 3, HEADS_PER_GROUP * PASS_DIM)]
    return jnp.concatenate(parts, axis=-1).reshape(d, 3 * GROUP_WIDTH)


def kernel(x, positions, mix_norm_g, w_in, gmlp_norm_g, w_spatial, b_spatial, w_branch_a,
           w_branch_b, w_out, ffn_norm_g, w_up, conv_w, conv_b, w_down, final_norm_g):
    batch, seq, d = x.shape
    assert w_in.shape[0] == 1, "single-layer block: the final norm is fused into the FFN stage"
    layer = 0
    n = batch * seq
    inv_freq = ROPE_THETA ** (-jnp.arange(0, ROT_DIM, 2, dtype=F32) / ROT_DIM)
    inv_freq = jnp.broadcast_to(inv_freq[:, None], (ROT_HALF, LANES))
    pos2 = positions.reshape(n // PROJ_TILE, 1, PROJ_TILE)
    xf = x.reshape(n, d)
    wi = w_in[layer]
    wi = jnp.concatenate([wi[:, :OFF_Q], _rotary_first(wi[:, OFF_Q:OFF_K]) * HEAD_DIM ** -0.5,
                          _rotary_first(wi[:, OFF_K:OFF_V]) * np.float32(np.log2(np.e)),
                          wi[:, OFF_V:]], axis=1).astype(BF16)
    bias_tbl = jnp.repeat(b_spatial[layer].T, GMLP_WIDTH // GMLP_GROUPS, axis=1)
    res = _projection(xf, pos2, mix_norm_g[layer].reshape(1, d), wi,
                      gmlp_norm_g[layer].reshape(1, GMLP_WIDTH), w_spatial[layer], bias_tbl,
                      w_branch_a[layer].astype(BF16), inv_freq)
    qs, ks, vs, ta, g1 = res[0:3], res[3:6], res[6:9], res[9], res[10]
    outs, lses = [], []
    for g, (dil, rps) in enumerate(zip(DILATIONS, RESIDUES_PER_STEP)):
        o, lse = _attention(qs[g], ks[g], vs[g], batch, seq, dil, rps)
        outs.append(o)
        lses.append(lse)
    out = _ffn(xf, ta, g1, outs, lses, w_branch_b[layer].astype(BF16), w_out[layer].astype(BF16),
               ffn_norm_g[layer].reshape(1, d), w_up[layer].astype(BF16), conv_w[layer],
               conv_b[layer].reshape(1, D_FF), w_down[layer].astype(BF16),
               final_norm_g.reshape(1, d), batch, seq)
    return out.reshape(batch, seq, d)
```

```python
import functools

import jax
import jax.numpy as jnp
import numpy as np
from jax import lax
from jax.experimental import pallas as pl
from jax.experimental.pallas import tpu as pltpu

F32 = jnp.float32
BF16 = jnp.bfloat16

D_MODEL = 1024
EPS = 1e-6
GMLP_WIDTH = 768
GMLP_GROUPS = 4
GMLP_CHUNK = 128
HEAD_DIM = 64
HEADS_PER_GROUP = 4
GROUP_WIDTH = HEADS_PER_GROUP * HEAD_DIM
DILATIONS = (1, 4, 16)
RESIDUES_PER_STEP = (1, 4, 16)
BAND = 128
ATTN_BLOCK = 128
ATTN_GROUP = 8
ROPE_THETA = 500000.0
ROT_DIM = HEAD_DIM // 4
ROT_HALF = ROT_DIM // 2
PASS_DIM = HEAD_DIM - ROT_DIM
D_FF = 2816
FF_CHUNK = 256
N_FF_CHUNKS = D_FF // FF_CHUNK

OFF_Q = 2 * GMLP_WIDTH
OFF_K = OFF_Q + 3 * GROUP_WIDTH
OFF_V = OFF_K + 3 * GROUP_WIDTH
OFF_G0 = OFF_V + 3 * GROUP_WIDTH
OFF_G1 = OFF_G0 + D_MODEL
IN_WIDTH = OFF_G1 + D_MODEL

LANES = 128
PROJ_TILE = 512
PROJ_SUBTILE = 256
FFN_TILE = 512
FFN_SUBTILE = 256
VMEM_LIMIT_BYTES = 56 * 1024 * 1024


def _erf(x):
    return lax.erf(x)


def _gelu(x):
    return 0.5 * x * (1.0 + _erf(x * np.float32(np.sqrt(0.5))))


def _sigmoid(x):
    return 1.0 / (1.0 + jnp.exp(-x))


def _rms(x, g):
    ms = jnp.mean(x * x, axis=-1, keepdims=True)
    return x * lax.rsqrt(ms + EPS) * g


def _proj_kernel(x_ref, pos_ref, g_ref, w_ref, gg_ref, ws_ref, bias_ref, pa_ref, invf_ref,
                 q0_ref, q1_ref, q2_ref, k0_ref, k1_ref, k2_ref, v0_ref, v1_ref, v2_ref,
                 ta_ref, g1_ref, stage_ref):
    tm = x_ref.shape[0]
    rows = PROJ_SUBTILE
    subs = range(tm // rows)
    q_refs, k_refs, v_refs = (q0_ref, q1_ref, q2_ref), (k0_ref, k1_ref, k2_ref), (v0_ref, v1_ref, v2_ref)
    row = lax.broadcasted_iota(jnp.int32, (GMLP_CHUNK, GMLP_CHUNK), 0)
    col = lax.broadcasted_iota(jnp.int32, (GMLP_CHUNK, GMLP_CHUNK), 1)
    wm = [jnp.where(row >= col, ws_ref[g], 0.0).astype(BF16) for g in range(GMLP_GROUPS)]
    low_half = lax.broadcasted_iota(jnp.int32, (GMLP_CHUNK, LANES), 1) < (LANES // 2)

    def proj(h, lo, hi):
        return jnp.dot(h, w_ref[:, lo:hi], preferred_element_type=F32)

    def gating_inputs(sub):
        h = _rms(x_ref[sub * rows:(sub + 1) * rows], g_ref[...]).astype(BF16)
        vn = _rms(_gelu(proj(h, GMLP_WIDTH, 2 * GMLP_WIDTH)), gg_ref[...]).astype(BF16)
        u = _gelu(proj(h, 0, GMLP_WIDTH))
        return h, u, vn

    def store_by_residue(sub, halves, ref, g, slot):
        dil = DILATIONS[g]
        out_rows = slice(sub * rows // dil, (sub + 1) * rows // dil)
        if dil == 1:
            for s, half in enumerate(halves):
                ref[out_rows, s * LANES:(s + 1) * LANES] = half.astype(BF16)
            return
        for s, half in enumerate(halves):
            stage_ref[sub, slot, s] = half
        for r in range(dil):
            for s in range(2):
                lo = r * GROUP_WIDTH + s * LANES
                ref[out_rows, lo:lo + LANES] = stage_ref[
                    sub, slot, s, pl.ds(r, rows // dil, stride=dil), :].astype(BF16)

    def gates_and_qkv(sub, h):
        g0 = _sigmoid(proj(h, OFF_G0, OFF_G1))
        g1_ref[sub * rows:(sub + 1) * rows] = _sigmoid(proj(h, OFF_G1, IN_WIDTH)).astype(BF16)
        pos = pos_ref[:, sub * rows:(sub + 1) * rows].astype(F32)
        ang = jnp.concatenate([invf_ref[...] * pos[:, b * LANES:(b + 1) * LANES]
                               for b in range(rows // LANES)], axis=1)
        n_rot = 2 * HEADS_PER_GROUP
        cos = jnp.concatenate([jnp.cos(ang)] * n_rot + [jnp.ones((LANES - 8 * n_rot, rows), F32)], axis=0).T
        sin = jnp.concatenate([jnp.sin(ang)] * n_rot + [jnp.zeros((LANES - 8 * n_rot, rows), F32)], axis=0).T
        lane = lax.broadcasted_iota(jnp.int32, (rows, LANES), 1)
        sin_a = jnp.where(lane < 32, -sin, 0.0)
        sin_b = jnp.where(lane < 32, 0.0, sin)
        for off, refs, slot0 in ((OFF_Q, q_refs, 0), (OFF_K, k_refs, 2)):
            for g, ref in enumerate(refs):
                t = proj(h, off + g * GROUP_WIDTH, off + (g + 1) * GROUP_WIDTH)
                t0 = t[:, :LANES]
                t0 = t0 * cos + pltpu.roll(t0, LANES - 32, 1) * sin_a + pltpu.roll(t0, 32, 1) * sin_b
                store_by_residue(sub, (t0, t[:, LANES:]), ref, g, slot0 + g - 1)
        for g, ref in enumerate(v_refs):
            t = proj(h, OFF_V + g * GROUP_WIDTH, OFF_V + (g + 1) * GROUP_WIDTH)
            store_by_residue(sub, (t[:, :LANES], t[:, LANES:]), ref, g, 4 + g - 1)
        return g0

    def spatial_gating(sub, u, vn, g0):
        bias = bias_ref[...]
        ya_rows = []
        for c in range(rows // GMLP_CHUNK):
            vc = vn[c * GMLP_CHUNK:(c + 1) * GMLP_CHUNK]
            p0 = jnp.dot(wm[0], vc[:, 0:256], preferred_element_type=F32)
            p1 = jnp.dot(wm[1], vc[:, 128:384], preferred_element_type=F32)
            p2 = jnp.dot(wm[2], vc[:, 384:640], preferred_element_type=F32)
            p3 = jnp.dot(wm[3], vc[:, 512:768], preferred_element_type=F32)
            mixed = jnp.concatenate(
                [p0[:, :LANES], jnp.where(low_half, p0[:, LANES:], p1[:, :LANES]), p1[:, LANES:],
                 p2[:, :LANES], jnp.where(low_half, p2[:, LANES:], p3[:, :LANES]), p3[:, LANES:]],
                axis=1) + bias
            ya_rows.append(u[c * GMLP_CHUNK:(c + 1) * GMLP_CHUNK] * mixed)
        ya = jnp.concatenate(ya_rows, axis=0).astype(BF16)
        ta_ref[sub * rows:(sub + 1) * rows] = (
            g0 * jnp.dot(ya, pa_ref[...], preferred_element_type=F32)).astype(BF16)

    chains = [gating_inputs(sub) for sub in subs]
    gates = [gates_and_qkv(sub, chains[sub][0]) for sub in subs]
    for sub in subs:
        spatial_gating(sub, chains[sub][1], chains[sub][2], gates[sub])


def _resident(shape):
    nd = len(shape)
    return pl.BlockSpec(shape, lambda *_: (0,) * nd, pipeline_mode=pl.Buffered(1))


def _projection(x2, pos2, mix_g, w_in, gmlp_g, w_s, bias_tbl, w_a, inv_freq):
    n = x2.shape[0]
    tm = PROJ_TILE
    row_spec = lambda w: pl.BlockSpec((tm, w), lambda i: (i, 0))
    grp_specs = [pl.BlockSpec((tm // d, d * GROUP_WIDTH), lambda i: (i, 0)) for d in DILATIONS] * 3
    grp_shapes = [jax.ShapeDtypeStruct((n // d, d * GROUP_WIDTH), BF16) for d in DILATIONS] * 3
    wide = jax.ShapeDtypeStruct((n, D_MODEL), BF16)
    return pl.pallas_call(
        _proj_kernel,
        grid=(n // tm,),
        in_specs=[row_spec(D_MODEL), pl.BlockSpec((None, 1, tm), lambda i: (i, 0, 0)),
                  _resident((1, D_MODEL)),
                  _resident((D_MODEL, IN_WIDTH)), _resident((1, GMLP_WIDTH)),
                  _resident((GMLP_GROUPS, GMLP_CHUNK, GMLP_CHUNK)),
                  _resident((GMLP_CHUNK, GMLP_WIDTH)), _resident((GMLP_WIDTH, D_MODEL)),
                  _resident((ROT_HALF, LANES))],
        out_specs=grp_specs + [row_spec(D_MODEL)] * 2,
        out_shape=grp_shapes + [wide] * 2,
        scratch_shapes=[pltpu.VMEM((tm // PROJ_SUBTILE, 6, 2, PROJ_SUBTILE, LANES), F32)],
        compiler_params=pltpu.CompilerParams(
            dimension_semantics=("arbitrary",), vmem_limit_bytes=VMEM_LIMIT_BYTES),
        name="projection",
    )(x2, pos2, mix_g, w_in, gmlp_g, w_s, bias_tbl, w_a, inv_freq)


def _head_masks():
    lane = lax.broadcasted_iota(jnp.int32, (1, GROUP_WIDTH), 1)
    qk_head = jnp.where(lane < 2 * 32, (lane % 32) // ROT_HALF, (lane - 2 * 32) // PASS_DIM)
    v_head = lane // HEAD_DIM
    return qk_head, v_head


def _attn_q_masks():
    qk_head, _ = _head_masks()
    heads = jnp.arange(HEADS_PER_GROUP, dtype=jnp.int32)[:, None, None]
    mask = jnp.where(qk_head[None] == heads, 1.0, 0.0)
    return jnp.broadcast_to(mask, (HEADS_PER_GROUP, ATTN_BLOCK, GROUP_WIDTH)).astype(BF16)


def _attn_kernel(q_ref, k_ref, v_ref, qm_ref, *rest, n_res, n_blocks, group, n_casts):
    o_ref, lse_ref = rest[n_casts:n_casts + 2]
    for src, dst in zip(rest[:n_casts], rest[n_casts + 2:]):
        dst[...] = src[...].astype(BF16)
    nh = HEADS_PER_GROUP
    blk = ATTN_BLOCK
    _, v_head = _head_masks()
    qi = lax.broadcasted_iota(jnp.int32, (blk, 2 * blk), 0)
    ki = lax.broadcasted_iota(jnp.int32, (blk, 2 * blk), 1)
    band2 = jnp.where((ki >= qi) & (ki <= qi + BAND), 0.0, -jnp.inf).astype(F32)
    band1 = band2[:, blk:]
    band2 = jnp.concatenate([band2] * nh, axis=0)
    band1 = jnp.concatenate([band1] * nh, axis=0)
    low_half = lax.broadcasted_iota(jnp.int32, (blk, LANES), 1) < HEAD_DIM

    def per_head(x):
        halves = [jnp.where(low_half, x[2 * c * blk:(2 * c + 1) * blk], x[(2 * c + 1) * blk:(2 * c + 2) * blk])
                  for c in range(nh // 2)]
        return jnp.concatenate(halves, axis=1)

    def scores(cols, q_rows, kv_rows, band):
        qb = q_ref[q_rows, cols]
        qs = jnp.concatenate([qb * qm_ref[h] for h in range(nh)], axis=0)
        return lax.dot_general(qs, k_ref[kv_rows, cols], (((1,), (1,)), ((), ())),
                               preferred_element_type=F32) + band

    def softmax(s):
        m = jnp.max(s, axis=-1, keepdims=True)
        p = jnp.exp2(s - m)
        return p.astype(BF16), m, jnp.sum(p, axis=-1, keepdims=True)

    def finish(cols, q_rows, kv_rows, p, m, den):
        pv = jnp.dot(p, v_ref[kv_rows, cols], preferred_element_type=F32)
        out = pv[(nh - 1) * blk:]
        for h in range(nh - 2, -1, -1):
            out = jnp.where(v_head == h, pv[h * blk:(h + 1) * blk], out)
        den = per_head(jnp.broadcast_to(den, (nh * blk, LANES)))
        o_ref[q_rows, cols] = (out / den).astype(o_ref.dtype)
        m_wide = per_head(jnp.broadcast_to(m, (nh * blk, LANES)))
        lse_ref[q_rows, cols] = (m_wide + jnp.log2(den)) * np.float32(np.log(2.0))

    def block_group(cols, blocks):
        s = [scores(cols, *b) for b in blocks]
        pmd = [softmax(x) for x in s]
        for b, (p, m, den) in zip(blocks, pmd):
            finish(cols, b[0], b[1], p, m, den)

    def later_block(q0):
        return slice(q0, q0 + blk), slice(q0 - blk, q0 + blk), band2

    for r in range(n_res):
        cols = slice(r * GROUP_WIDTH, (r + 1) * GROUP_WIDTH)
        first = [(slice(0, blk), slice(0, blk), band1)]
        block_group(cols, first + [later_block(j * blk) for j in range(1, group)])
        for i in range(1, n_blocks // group):
            block_group(cols, [later_block(j * blk) for j in range(i * group, (i + 1) * group)])


def _attention(q, k, v, batch, seq, dil, res_per_step, casts):
    sub = seq // dil
    width = res_per_step * GROUP_WIDTH
    view = lambda t: t.reshape(batch, sub, dil * GROUP_WIDTH)
    spec = pl.BlockSpec((None, sub, width), lambda b, r: (b, 0, r))
    n_blocks = sub // ATTN_BLOCK
    qm = _attn_q_masks()
    cast_specs = [pl.BlockSpec((w.shape[0] // batch, w.shape[1]), lambda b, r: (b, 0)) for w in casts]
    kern = functools.partial(_attn_kernel, n_res=res_per_step, n_blocks=n_blocks,
                             group=min(ATTN_GROUP, n_blocks), n_casts=len(casts))
    o, lse, *cast = pl.pallas_call(
        kern,
        grid=(batch, dil // res_per_step),
        in_specs=[spec, spec, spec, _resident(qm.shape)] + cast_specs,
        out_specs=[spec, spec] + cast_specs,
        out_shape=[jax.ShapeDtypeStruct((batch, sub, dil * GROUP_WIDTH), BF16),
                   jax.ShapeDtypeStruct((batch, sub, dil * GROUP_WIDTH), F32)]
                  + [jax.ShapeDtypeStruct(w.shape, BF16) for w in casts],
        compiler_params=pltpu.CompilerParams(
            dimension_semantics=("arbitrary", "arbitrary"), vmem_limit_bytes=VMEM_LIMIT_BYTES),
        name=f"attention_d{dil}",
    )(view(q), view(k), view(v), qm, *casts)
    return o.reshape(batch * sub, dil * GROUP_WIDTH), lse.reshape(batch * sub, dil * GROUP_WIDTH), cast


def _ffn_kernel(x_ref, ta_ref, g1_ref, o0_ref, o1_ref, o2_ref, l0_ref, l1_ref, l2_ref,
                pb_ref, wo_ref, fg_ref, wup_ref, cw_ref, cb_ref, wd_ref, og_ref,
                out_ref, carry_ref, act_ref, stage_ref):
    tm = x_ref.shape[0]
    rows = FFN_SUBTILE
    subs = range(tm // rows)

    @pl.when(pl.program_id(1) == 0)
    def _():
        carry_ref[...] = jnp.zeros_like(carry_ref)

    def natural(sub, ref, g, slot):
        dil = DILATIONS[g]
        view_rows = slice(sub * rows // dil, (sub + 1) * rows // dil)
        if dil == 1:
            return ref[view_rows, :].astype(F32)
        for r in range(dil):
            for s in range(2):
                lo = r * GROUP_WIDTH + s * LANES
                stage_ref[sub, slot, s, pl.ds(r, rows // dil, stride=dil), :] = (
                    ref[view_rows, lo:lo + LANES].astype(F32))
        return jnp.concatenate([stage_ref[sub, slot, 0], stage_ref[sub, slot, 1]], axis=1)

    def merged_branches(sub):
        tok = slice(sub * rows, (sub + 1) * rows)
        l0, l1, l2 = natural(sub, l0_ref, 0, 0), natural(sub, l1_ref, 1, 0), natural(sub, l2_ref, 2, 1)
        lmax = jnp.maximum(jnp.maximum(l0, l1), l2)
        e0, e1, e2 = jnp.exp(l0 - lmax), jnp.exp(l1 - lmax), jnp.exp(l2 - lmax)
        yb = (e0 * natural(sub, o0_ref, 0, 0) + e1 * natural(sub, o1_ref, 1, 2)
              + e2 * natural(sub, o2_ref, 2, 3)) / (e0 + e1 + e2)
        merged = ta_ref[tok, :].astype(F32) + g1_ref[tok, :].astype(F32) * jnp.dot(
            yb.astype(BF16), pb_ref[...], preferred_element_type=F32)
        return merged.astype(BF16)

    def out_proj(sub, merged):
        tok = slice(sub * rows, (sub + 1) * rows)
        x1 = x_ref[tok, :] + jnp.dot(merged, wo_ref[...], preferred_element_type=F32)
        return x1, _rms(x1, fg_ref[...]).astype(BF16)

    row8 = lax.broadcasted_iota(jnp.int32, (8, FF_CHUNK), 0)

    def up_chunk(sub, c, h2, prev):
        cols = slice(c * FF_CHUNK, (c + 1) * FF_CHUNK)
        a = jnp.dot(h2, wup_ref[:, cols], preferred_element_type=F32)
        val = jnp.dot(h2, wup_ref[:, D_FF + c * FF_CHUNK:D_FF + (c + 1) * FF_CHUNK],
                      preferred_element_type=F32)
        s1 = pltpu.roll(a, 1, 0)
        s2 = pltpu.roll(a, 2, 0)
        s1 = jnp.concatenate([jnp.where(row8 < 1, pltpu.roll(prev, 1, 0), s1[:8]), s1[8:]], axis=0)
        s2 = jnp.concatenate([jnp.where(row8 < 2, pltpu.roll(prev, 2, 0), s2[:8]), s2[8:]], axis=0)
        w = cw_ref[:, cols]
        y = s2 * w[0:1] + s1 * w[1:2] + a * w[2:3] + cb_ref[:, cols]
        act_ref[sub * rows:(sub + 1) * rows, cols] = (_gelu(y) * val).astype(BF16)
        return a[rows - 8:]

    merged = [merged_branches(sub) for sub in subs]
    xh = [out_proj(sub, merged[sub]) for sub in subs]
    for c in range(N_FF_CHUNKS):
        cols = slice(c * FF_CHUNK, (c + 1) * FF_CHUNK)
        prev = carry_ref[:, cols]
        for sub in subs:
            prev = up_chunk(sub, c, xh[sub][1], prev)
        carry_ref[:, cols] = prev
    for sub in subs:
        tok = slice(sub * rows, (sub + 1) * rows)
        x2 = xh[sub][0] + jnp.dot(act_ref[tok, :], wd_ref[...], preferred_element_type=F32)
        out_ref[tok, :] = _rms(x2, og_ref[...])


def _ffn(x, ta, g1, os_, ls_, w_b, w_out, ffn_g, w_up, conv_w, conv_b, w_down, final_g, batch, seq):
    tm = FFN_TILE
    tiles = seq // tm
    row_spec = lambda w: pl.BlockSpec((tm, w), lambda b, i: (b * tiles + i, 0))
    grp_specs = [pl.BlockSpec((tm // d, d * GROUP_WIDTH), lambda b, i: (b * tiles + i, 0))
                 for d in DILATIONS]
    return pl.pallas_call(
        _ffn_kernel,
        grid=(batch, tiles),
        in_specs=[row_spec(D_MODEL), row_spec(D_MODEL), row_spec(D_MODEL)]
                 + grp_specs * 2
                 + [_resident((GROUP_WIDTH, D_MODEL)), _resident((D_MODEL, D_MODEL)),
                    _resident((1, D_MODEL)), _resident((D_MODEL, 2 * D_FF)),
                    _resident((3, D_FF)), _resident((1, D_FF)), _resident((D_FF, D_MODEL)),
                    _resident((1, D_MODEL))],
        out_specs=row_spec(D_MODEL),
        out_shape=jax.ShapeDtypeStruct((batch * seq, D_MODEL), F32),
        scratch_shapes=[pltpu.VMEM((8, D_FF), F32), pltpu.VMEM((tm, D_FF), BF16),
                        pltpu.VMEM((tm // FFN_SUBTILE, 4, 2, FFN_SUBTILE, LANES), F32)],
        compiler_params=pltpu.CompilerParams(
            dimension_semantics=("arbitrary", "arbitrary"), vmem_limit_bytes=VMEM_LIMIT_BYTES),
        name="mix_ffn",
    )(x, ta, g1, *os_, *ls_, w_b, w_out, ffn_g, w_up, conv_w, conv_b, w_down, final_g)


def _rotary_first(w):
    d = w.shape[0]
    t = w.reshape(d, 3, HEADS_PER_GROUP, HEAD_DIM)
    parts = [t[..., :ROT_HALF].reshape(d, 3, HEADS_PER_GROUP * ROT_HALF),
             t[..., ROT_HALF:ROT_DIM].reshape(d, 3, HEADS_PER_GROUP * ROT_HALF),
             t[..., ROT_DIM:].reshape(d, 3, HEADS_PER_GROUP * PASS_DIM)]
    return jnp.concatenate(parts, axis=-1).reshape(d, 3 * GROUP_WIDTH)


def kernel(x, positions, mix_norm_g, w_in, gmlp_norm_g, w_spatial, b_spatial, w_branch_a,
           w_branch_b, w_out, ffn_norm_g, w_up, conv_w, conv_b, w_down, final_norm_g):
    batch, seq, d = x.shape
    assert w_in.shape[0] == 1, "single-layer block: the final norm is fused into the FFN stage"
    layer = 0
    n = batch * seq
    inv_freq = ROPE_THETA ** (-jnp.arange(0, ROT_DIM, 2, dtype=F32) / ROT_DIM)
    inv_freq = jnp.broadcast_to(inv_freq[:, None], (ROT_HALF, LANES))
    pos2 = positions.reshape(n // PROJ_TILE, 1, PROJ_TILE)
    xf = x.reshape(n, d)
    wi = w_in[layer]
    wi = jnp.concatenate([wi[:, :OFF_Q], _rotary_first(wi[:, OFF_Q:OFF_K]) * HEAD_DIM ** -0.5,
                          _rotary_first(wi[:, OFF_K:OFF_V]) * np.float32(np.log2(np.e)),
                          wi[:, OFF_V:]], axis=1).astype(BF16)
    bias_tbl = jnp.repeat(b_spatial[layer].T, GMLP_WIDTH // GMLP_GROUPS, axis=1)
    res = _projection(xf, pos2, mix_norm_g[layer].reshape(1, d), wi,
                      gmlp_norm_g[layer].reshape(1, GMLP_WIDTH), w_spatial[layer], bias_tbl,
                      w_branch_a[layer].astype(BF16), inv_freq)
    qs, ks, vs, ta, g1 = res[0:3], res[3:6], res[6:9], res[9], res[10]
    side_casts = ([w_up[layer]], [w_down[layer]], [w_out[layer], w_branch_b[layer]])
    outs, lses, cast = [], [], []
    for g, (dil, rps) in enumerate(zip(DILATIONS, RESIDUES_PER_STEP)):
        o, lse, c = _attention(qs[g], ks[g], vs[g], batch, seq, dil, rps, side_casts[g])
        outs.append(o)
        lses.append(lse)
        cast += c
    w_up_b, w_down_b, w_out_b, w_b_b = cast
    out = _ffn(xf, ta, g1, outs, lses, w_b_b, w_out_b,
               ffn_norm_g[layer].reshape(1, d), w_up_b, conv_w[layer],
               conv_b[layer].reshape(1, D_FF), w_down_b,
               final_norm_g.reshape(1, d), batch, seq)
    return out.reshape(batch, seq, d)
```

```python
import functools

import jax
import jax.numpy as jnp
import numpy as np
from jax import lax
from jax.experimental import pallas as pl
from jax.experimental.pallas import tpu as pltpu

F32 = jnp.float32
BF16 = jnp.bfloat16

D_MODEL = 1024
EPS = 1e-6
GMLP_WIDTH = 768
GMLP_GROUPS = 4
GMLP_CHUNK = 128
HEAD_DIM = 64
HEADS_PER_GROUP = 4
GROUP_WIDTH = HEADS_PER_GROUP * HEAD_DIM
DILATIONS = (1, 4, 16)
RESIDUES_PER_STEP = (1, 4, 16)
BAND = 128
ATTN_BLOCK = 128
ATTN_GROUP = 8
ROPE_THETA = 500000.0
ROT_DIM = HEAD_DIM // 4
ROT_HALF = ROT_DIM // 2
PASS_DIM = HEAD_DIM - ROT_DIM
D_FF = 2816
FF_CHUNK = 256
N_FF_CHUNKS = D_FF // FF_CHUNK

OFF_Q = 2 * GMLP_WIDTH
OFF_K = OFF_Q + 3 * GROUP_WIDTH
OFF_V = OFF_K + 3 * GROUP_WIDTH
OFF_G0 = OFF_V + 3 * GROUP_WIDTH
OFF_G1 = OFF_G0 + D_MODEL
IN_WIDTH = OFF_G1 + D_MODEL

LANES = 128
PROJ_TILE = 512
PROJ_SUBTILE = 256
FFN_TILE = 512
FFN_SUBTILE = 256
VMEM_LIMIT_BYTES = 56 * 1024 * 1024


def _erf(x):
    return lax.erf(x)


def _gelu(x):
    return 0.5 * x * (1.0 + _erf(x * np.float32(np.sqrt(0.5))))


def _sigmoid(x):
    return 1.0 / (1.0 + jnp.exp(-x))


def _rms(x, g):
    ms = jnp.mean(x * x, axis=-1, keepdims=True)
    return x * lax.rsqrt(ms + EPS) * g


def _proj_kernel(x_ref, pos_ref, g_ref, w_ref, gg_ref, ws_ref, bias_ref, pa_ref, invf_ref, perm_ref,
                 q0_ref, q1_ref, q2_ref, k0_ref, k1_ref, k2_ref, v0_ref, v1_ref, v2_ref,
                 ta_ref, g1_ref, stage_ref, wqk_ref):
    tm = x_ref.shape[0]
    rows = PROJ_SUBTILE
    subs = range(tm // rows)

    @pl.when(pl.program_id(0) == 0)
    def _():
        for j in range(2 * len(DILATIONS)):
            cols = slice(j * GROUP_WIDTH, (j + 1) * GROUP_WIDTH)
            wqk_ref[:, cols] = jnp.dot(w_ref[:, OFF_Q + j * GROUP_WIDTH:OFF_Q + (j + 1) * GROUP_WIDTH],
                                       perm_ref[...], preferred_element_type=F32).astype(BF16)

    q_refs, k_refs, v_refs = (q0_ref, q1_ref, q2_ref), (k0_ref, k1_ref, k2_ref), (v0_ref, v1_ref, v2_ref)
    row = lax.broadcasted_iota(jnp.int32, (GMLP_CHUNK, GMLP_CHUNK), 0)
    col = lax.broadcasted_iota(jnp.int32, (GMLP_CHUNK, GMLP_CHUNK), 1)
    wm = [jnp.where(row >= col, ws_ref[g], 0.0).astype(BF16) for g in range(GMLP_GROUPS)]
    low_half = lax.broadcasted_iota(jnp.int32, (GMLP_CHUNK, LANES), 1) < (LANES // 2)

    def proj(h, lo, hi):
        return jnp.dot(h, w_ref[:, lo:hi], preferred_element_type=F32)

    def gating_inputs(sub):
        h = _rms(x_ref[sub * rows:(sub + 1) * rows], g_ref[...]).astype(BF16)
        vn = _rms(_gelu(proj(h, GMLP_WIDTH, 2 * GMLP_WIDTH)), gg_ref[...]).astype(BF16)
        u = _gelu(proj(h, 0, GMLP_WIDTH))
        return h, u, vn

    def store_by_residue(sub, halves, ref, g, slot):
        dil = DILATIONS[g]
        out_rows = slice(sub * rows // dil, (sub + 1) * rows // dil)
        if dil == 1:
            for s, half in enumerate(halves):
                ref[out_rows, s * LANES:(s + 1) * LANES] = half.astype(BF16)
            return
        for s, half in enumerate(halves):
            stage_ref[sub, slot, s] = half
        for r in range(dil):
            for s in range(2):
                lo = r * GROUP_WIDTH + s * LANES
                ref[out_rows, lo:lo + LANES] = stage_ref[
                    sub, slot, s, pl.ds(r, rows // dil, stride=dil), :].astype(BF16)

    def gates_and_qkv(sub, h):
        g0 = _sigmoid(proj(h, OFF_G0, OFF_G1))
        g1_ref[sub * rows:(sub + 1) * rows] = _sigmoid(proj(h, OFF_G1, IN_WIDTH)).astype(BF16)
        pos = pos_ref[:, sub * rows:(sub + 1) * rows].astype(F32)
        ang = jnp.concatenate([invf_ref[...] * pos[:, b * LANES:(b + 1) * LANES]
                               for b in range(rows // LANES)], axis=1)
        n_rot = 2 * HEADS_PER_GROUP
        cos = jnp.concatenate([jnp.cos(ang)] * n_rot + [jnp.ones((LANES - 8 * n_rot, rows), F32)], axis=0).T
        sin = jnp.concatenate([jnp.sin(ang)] * n_rot + [jnp.zeros((LANES - 8 * n_rot, rows), F32)], axis=0).T
        lane = lax.broadcasted_iota(jnp.int32, (rows, LANES), 1)
        sin_a = jnp.where(lane < 32, -sin, 0.0)
        sin_b = jnp.where(lane < 32, 0.0, sin)
        for off, refs, slot0 in ((OFF_Q, q_refs, 0), (OFF_K, k_refs, 2)):
            for g, ref in enumerate(refs):
                lo = off - OFF_Q + g * GROUP_WIDTH
                t = jnp.dot(h, wqk_ref[:, lo:lo + GROUP_WIDTH], preferred_element_type=F32)
                t0 = t[:, :LANES]
                t0 = t0 * cos + pltpu.roll(t0, LANES - 32, 1) * sin_a + pltpu.roll(t0, 32, 1) * sin_b
                store_by_residue(sub, (t0, t[:, LANES:]), ref, g, slot0 + g - 1)
        for g, ref in enumerate(v_refs):
            t = proj(h, OFF_V + g * GROUP_WIDTH, OFF_V + (g + 1) * GROUP_WIDTH)
            store_by_residue(sub, (t[:, :LANES], t[:, LANES:]), ref, g, 4 + g - 1)
        return g0

    def spatial_gating(sub, u, vn, g0):
        bias = bias_ref[...]
        ya_rows = []
        for c in range(rows // GMLP_CHUNK):
            vc = vn[c * GMLP_CHUNK:(c + 1) * GMLP_CHUNK]
            p0 = jnp.dot(wm[0], vc[:, 0:256], preferred_element_type=F32)
            p1 = jnp.dot(wm[1], vc[:, 128:384], preferred_element_type=F32)
            p2 = jnp.dot(wm[2], vc[:, 384:640], preferred_element_type=F32)
            p3 = jnp.dot(wm[3], vc[:, 512:768], preferred_element_type=F32)
            mixed = jnp.concatenate(
                [p0[:, :LANES], jnp.where(low_half, p0[:, LANES:], p1[:, :LANES]), p1[:, LANES:],
                 p2[:, :LANES], jnp.where(low_half, p2[:, LANES:], p3[:, :LANES]), p3[:, LANES:]],
                axis=1) + bias
            ya_rows.append(u[c * GMLP_CHUNK:(c + 1) * GMLP_CHUNK] * mixed)
        ya = jnp.concatenate(ya_rows, axis=0).astype(BF16)
        ta_ref[sub * rows:(sub + 1) * rows] = (
            g0 * jnp.dot(ya, pa_ref[...], preferred_element_type=F32)).astype(BF16)

    chains = [gating_inputs(sub) for sub in subs]
    gates = [gates_and_qkv(sub, chains[sub][0]) for sub in subs]
    for sub in subs:
        spatial_gating(sub, chains[sub][1], chains[sub][2], gates[sub])


def _resident(shape):
    nd = len(shape)
    return pl.BlockSpec(shape, lambda *_: (0,) * nd, pipeline_mode=pl.Buffered(1))


def _rotary_first_permutation():
    perm = np.zeros((GROUP_WIDTH, GROUP_WIDTH), np.float32)
    for h in range(HEADS_PER_GROUP):
        for dim in range(HEAD_DIM):
            if dim < ROT_HALF:
                dst = ROT_HALF * h + dim
            elif dim < ROT_DIM:
                dst = HEADS_PER_GROUP * ROT_HALF + ROT_HALF * h + (dim - ROT_HALF)
            else:
                dst = HEADS_PER_GROUP * ROT_DIM + PASS_DIM * h + (dim - ROT_DIM)
            perm[h * HEAD_DIM + dim, dst] = 1.0
    return jnp.asarray(perm, BF16)


def _projection(x2, pos2, mix_g, w_in, gmlp_g, w_s, bias_tbl, w_a, inv_freq):
    n = x2.shape[0]
    tm = PROJ_TILE
    row_spec = lambda w: pl.BlockSpec((tm, w), lambda i: (i, 0))
    grp_specs = [pl.BlockSpec((tm // d, d * GROUP_WIDTH), lambda i: (i, 0)) for d in DILATIONS] * 3
    grp_shapes = [jax.ShapeDtypeStruct((n // d, d * GROUP_WIDTH), BF16) for d in DILATIONS] * 3
    wide = jax.ShapeDtypeStruct((n, D_MODEL), BF16)
    return pl.pallas_call(
        _proj_kernel,
        grid=(n // tm,),
        in_specs=[row_spec(D_MODEL), pl.BlockSpec((None, 1, tm), lambda i: (i, 0, 0)),
                  _resident((1, D_MODEL)),
                  _resident((D_MODEL, IN_WIDTH)), _resident((1, GMLP_WIDTH)),
                  _resident((GMLP_GROUPS, GMLP_CHUNK, GMLP_CHUNK)),
                  _resident((GMLP_CHUNK, GMLP_WIDTH)), _resident((GMLP_WIDTH, D_MODEL)),
                  _resident((ROT_HALF, LANES)), _resident((GROUP_WIDTH, GROUP_WIDTH))],
        out_specs=grp_specs + [row_spec(D_MODEL)] * 2,
        out_shape=grp_shapes + [wide] * 2,
        scratch_shapes=[pltpu.VMEM((tm // PROJ_SUBTILE, 6, 2, PROJ_SUBTILE, LANES), F32),
                        pltpu.VMEM((D_MODEL, 2 * len(DILATIONS) * GROUP_WIDTH), BF16)],
        compiler_params=pltpu.CompilerParams(
            dimension_semantics=("arbitrary",), vmem_limit_bytes=VMEM_LIMIT_BYTES),
        name="projection",
    )(x2, pos2, mix_g, w_in, gmlp_g, w_s, bias_tbl, w_a, inv_freq, _rotary_first_permutation())


def _head_masks():
    lane = lax.broadcasted_iota(jnp.int32, (1, GROUP_WIDTH), 1)
    qk_head = jnp.where(lane < 2 * 32, (lane % 32) // ROT_HALF, (lane - 2 * 32) // PASS_DIM)
    v_head = lane // HEAD_DIM
    return qk_head, v_head


def _attn_q_masks():
    qk_head, _ = _head_masks()
    heads = jnp.arange(HEADS_PER_GROUP, dtype=jnp.int32)[:, None, None]
    mask = jnp.where(qk_head[None] == heads, 1.0, 0.0)
    return jnp.broadcast_to(mask, (HEADS_PER_GROUP, ATTN_BLOCK, GROUP_WIDTH)).astype(BF16)


def _attn_kernel(q_ref, k_ref, v_ref, qm_ref, *rest, n_res, n_blocks, group, n_casts):
    o_ref, lse_ref = rest[n_casts:n_casts + 2]
    for src, dst in zip(rest[:n_casts], rest[n_casts + 2:]):
        dst[...] = src[...].astype(BF16)
    nh = HEADS_PER_GROUP
    blk = ATTN_BLOCK
    _, v_head = _head_masks()
    qi = lax.broadcasted_iota(jnp.int32, (blk, 2 * blk), 0)
    ki = lax.broadcasted_iota(jnp.int32, (blk, 2 * blk), 1)
    band2 = jnp.where((ki >= qi) & (ki <= qi + BAND), 0.0, -jnp.inf).astype(F32)
    band1 = band2[:, blk:]
    band2 = jnp.concatenate([band2] * nh, axis=0)
    band1 = jnp.concatenate([band1] * nh, axis=0)
    low_half = lax.broadcasted_iota(jnp.int32, (blk, LANES), 1) < HEAD_DIM

    def per_head(x):
        halves = [jnp.where(low_half, x[2 * c * blk:(2 * c + 1) * blk], x[(2 * c + 1) * blk:(2 * c + 2) * blk])
                  for c in range(nh // 2)]
        return jnp.concatenate(halves, axis=1)

    def scores(cols, q_rows, kv_rows, band):
        qb = q_ref[q_rows, cols]
        qs = jnp.concatenate([qb * qm_ref[h] for h in range(nh)], axis=0)
        return lax.dot_general(qs, k_ref[kv_rows, cols], (((1,), (1,)), ((), ())),
                               preferred_element_type=F32) + band

    def softmax(s):
        m = jnp.max(s, axis=-1, keepdims=True)
        p = jnp.exp2(s - m)
        return p.astype(BF16), m, jnp.sum(p, axis=-1, keepdims=True)

    def finish(cols, q_rows, kv_rows, p, m, den):
        pv = jnp.dot(p, v_ref[kv_rows, cols], preferred_element_type=F32)
        out = pv[(nh - 1) * blk:]
        for h in range(nh - 2, -1, -1):
            out = jnp.where(v_head == h, pv[h * blk:(h + 1) * blk], out)
        den = per_head(jnp.broadcast_to(den, (nh * blk, LANES)))
        o_ref[q_rows, cols] = (out / den).astype(o_ref.dtype)
        m_wide = per_head(jnp.broadcast_to(m, (nh * blk, LANES)))
        lse_ref[q_rows, cols] = (m_wide + jnp.log2(den)) * np.float32(np.log(2.0))

    def block_group(cols, blocks):
        s = [scores(cols, *b) for b in blocks]
        pmd = [softmax(x) for x in s]
        for b, (p, m, den) in zip(blocks, pmd):
            finish(cols, b[0], b[1], p, m, den)

    def later_block(q0):
        return slice(q0, q0 + blk), slice(q0 - blk, q0 + blk), band2

    for r in range(n_res):
        cols = slice(r * GROUP_WIDTH, (r + 1) * GROUP_WIDTH)
        first = [(slice(0, blk), slice(0, blk), band1)]
        block_group(cols, first + [later_block(j * blk) for j in range(1, group)])
        for i in range(1, n_blocks // group):
            block_group(cols, [later_block(j * blk) for j in range(i * group, (i + 1) * group)])


def _attention(q, k, v, batch, seq, dil, res_per_step, casts):
    sub = seq // dil
    width = res_per_step * GROUP_WIDTH
    view = lambda t: t.reshape(batch, sub, dil * GROUP_WIDTH)
    spec = pl.BlockSpec((None, sub, width), lambda b, r: (b, 0, r))
    n_blocks = sub // ATTN_BLOCK
    qm = _attn_q_masks()
    cast_specs = [pl.BlockSpec((w.shape[0] // batch, w.shape[1]), lambda b, r: (b, 0)) for w in casts]
    kern = functools.partial(_attn_kernel, n_res=res_per_step, n_blocks=n_blocks,
                             group=min(ATTN_GROUP, n_blocks), n_casts=len(casts))
    o, lse, *cast = pl.pallas_call(
        kern,
        grid=(batch, dil // res_per_step),
        in_specs=[spec, spec, spec, _resident(qm.shape)] + cast_specs,
        out_specs=[spec, spec] + cast_specs,
        out_shape=[jax.ShapeDtypeStruct((batch, sub, dil * GROUP_WIDTH), BF16),
                   jax.ShapeDtypeStruct((batch, sub, dil * GROUP_WIDTH), F32)]
                  + [jax.ShapeDtypeStruct(w.shape, BF16) for w in casts],
        compiler_params=pltpu.CompilerParams(
            dimension_semantics=("arbitrary", "arbitrary"), vmem_limit_bytes=VMEM_LIMIT_BYTES),
        name=f"attention_d{dil}",
    )(view(q), view(k), view(v), qm, *casts)
    return o.reshape(batch * sub, dil * GROUP_WIDTH), lse.reshape(batch * sub, dil * GROUP_WIDTH), cast


def _ffn_kernel(x_ref, ta_ref, g1_ref, o0_ref, o1_ref, o2_ref, l0_ref, l1_ref, l2_ref,
                pb_ref, wo_ref, fg_ref, wup_ref, cw_ref, cb_ref, wd_ref, og_ref,
                out_ref, carry_ref, act_ref, stage_ref):
    tm = x_ref.shape[0]
    rows = FFN_SUBTILE
    subs = range(tm // rows)

    @pl.when(pl.program_id(1) == 0)
    def _():
        carry_ref[...] = jnp.zeros_like(carry_ref)

    def natural(sub, ref, g, slot):
        dil = DILATIONS[g]
        view_rows = slice(sub * rows // dil, (sub + 1) * rows // dil)
        if dil == 1:
            return ref[view_rows, :].astype(F32)
        for r in range(dil):
            for s in range(2):
                lo = r * GROUP_WIDTH + s * LANES
                stage_ref[sub, slot, s, pl.ds(r, rows // dil, stride=dil), :] = (
                    ref[view_rows, lo:lo + LANES].astype(F32))
        return jnp.concatenate([stage_ref[sub, slot, 0], stage_ref[sub, slot, 1]], axis=1)

    def merged_branches(sub):
        tok = slice(sub * rows, (sub + 1) * rows)
        l0, l1, l2 = natural(sub, l0_ref, 0, 0), natural(sub, l1_ref, 1, 0), natural(sub, l2_ref, 2, 1)
        lmax = jnp.maximum(jnp.maximum(l0, l1), l2)
        e0, e1, e2 = jnp.exp(l0 - lmax), jnp.exp(l1 - lmax), jnp.exp(l2 - lmax)
        yb = (e0 * natural(sub, o0_ref, 0, 0) + e1 * natural(sub, o1_ref, 1, 2)
              + e2 * natural(sub, o2_ref, 2, 3)) / (e0 + e1 + e2)
        merged = ta_ref[tok, :].astype(F32) + g1_ref[tok, :].astype(F32) * jnp.dot(
            yb.astype(BF16), pb_ref[...], preferred_element_type=F32)
        return merged.astype(BF16)

    def out_proj(sub, merged):
        tok = slice(sub * rows, (sub + 1) * rows)
        x1 = x_ref[tok, :] + jnp.dot(merged, wo_ref[...], preferred_element_type=F32)
        return x1, _rms(x1, fg_ref[...]).astype(BF16)

    row8 = lax.broadcasted_iota(jnp.int32, (8, FF_CHUNK), 0)

    def up_chunk(sub, c, h2, prev):
        cols = slice(c * FF_CHUNK, (c + 1) * FF_CHUNK)
        a = jnp.dot(h2, wup_ref[:, cols], preferred_element_type=F32)
        val = jnp.dot(h2, wup_ref[:, D_FF + c * FF_CHUNK:D_FF + (c + 1) * FF_CHUNK],
                      preferred_element_type=F32)
        s1 = pltpu.roll(a, 1, 0)
        s2 = pltpu.roll(a, 2, 0)
        s1 = jnp.concatenate([jnp.where(row8 < 1, pltpu.roll(prev, 1, 0), s1[:8]), s1[8:]], axis=0)
        s2 = jnp.concatenate([jnp.where(row8 < 2, pltpu.roll(prev, 2, 0), s2[:8]), s2[8:]], axis=0)
        w = cw_ref[:, cols]
        y = s2 * w[0:1] + s1 * w[1:2] + a * w[2:3] + cb_ref[:, cols]
        act_ref[sub * rows:(sub + 1) * rows, cols] = (_gelu(y) * val).astype(BF16)
        return a[rows - 8:]

    merged = [merged_branches(sub) for sub in subs]
    xh = [out_proj(sub, merged[sub]) for sub in subs]
    for c in range(N_FF_CHUNKS):
        cols = slice(c * FF_CHUNK, (c + 1) * FF_CHUNK)
        prev = carry_ref[:, cols]
        for sub in subs:
            prev = up_chunk(sub, c, xh[sub][1], prev)
        carry_ref[:, cols] = prev
    for sub in subs:
        tok = slice(sub * rows, (sub + 1) * rows)
        x2 = xh[sub][0] + jnp.dot(act_ref[tok, :], wd_ref[...], preferred_element_type=F32)
        out_ref[tok, :] = _rms(x2, og_ref[...])


def _ffn(x, ta, g1, os_, ls_, w_b, w_out, ffn_g, w_up, conv_w, conv_b, w_down, final_g, batch, seq):
    tm = FFN_TILE
    tiles = seq // tm
    row_spec = lambda w: pl.BlockSpec((tm, w), lambda b, i: (b * tiles + i, 0))
    grp_specs = [pl.BlockSpec((tm // d, d * GROUP_WIDTH), lambda b, i: (b * tiles + i, 0))
                 for d in DILATIONS]
    return pl.pallas_call(
        _ffn_kernel,
        grid=(batch, tiles),
        in_specs=[row_spec(D_MODEL), row_spec(D_MODEL), row_spec(D_MODEL)]
                 + grp_specs * 2
                 + [_resident((GROUP_WIDTH, D_MODEL)), _resident((D_MODEL, D_MODEL)),
                    _resident((1, D_MODEL)), _resident((D_MODEL, 2 * D_FF)),
                    _resident((3, D_FF)), _resident((1, D_FF)), _resident((D_FF, D_MODEL)),
                    _resident((1, D_MODEL))],
        out_specs=row_spec(D_MODEL),
        out_shape=jax.ShapeDtypeStruct((batch * seq, D_MODEL), F32),
        scratch_shapes=[pltpu.VMEM((8, D_FF), F32), pltpu.VMEM((tm, D_FF), BF16),
                        pltpu.VMEM((tm // FFN_SUBTILE, 4, 2, FFN_SUBTILE, LANES), F32)],
        compiler_params=pltpu.CompilerParams(
            dimension_semantics=("arbitrary", "arbitrary"), vmem_limit_bytes=VMEM_LIMIT_BYTES),
        name="mix_ffn",
    )(x, ta, g1, *os_, *ls_, w_b, w_out, ffn_g, w_up, conv_w, conv_b, w_down, final_g)


def kernel(x, positions, mix_norm_g, w_in, gmlp_norm_g, w_spatial, b_spatial, w_branch_a,
           w_branch_b, w_out, ffn_norm_g, w_up, conv_w, conv_b, w_down, final_norm_g):
    batch, seq, d = x.shape
    assert w_in.shape[0] == 1, "single-layer block: the final norm is fused into the FFN stage"
    layer = 0
    n = batch * seq
    inv_freq = ROPE_THETA ** (-jnp.arange(0, ROT_DIM, 2, dtype=F32) / ROT_DIM)
    inv_freq = jnp.broadcast_to(inv_freq[:, None], (ROT_HALF, LANES))
    pos2 = positions.reshape(n // PROJ_TILE, 1, PROJ_TILE)
    xf = x.reshape(n, d)
    col_scale = np.ones((IN_WIDTH,), np.float32)
    col_scale[OFF_Q:OFF_K] = HEAD_DIM ** -0.5
    col_scale[OFF_K:OFF_V] = np.log2(np.e)
    wi = (w_in[layer] * col_scale).astype(BF16)
    bias_tbl = jnp.repeat(b_spatial[layer].T, GMLP_WIDTH // GMLP_GROUPS, axis=1)
    res = _projection(xf, pos2, mix_norm_g[layer].reshape(1, d), wi,
                      gmlp_norm_g[layer].reshape(1, GMLP_WIDTH), w_spatial[layer], bias_tbl,
                      w_branch_a[layer].astype(BF16), inv_freq)
    qs, ks, vs, ta, g1 = res[0:3], res[3:6], res[6:9], res[9], res[10]
    side_casts = ([w_up[layer]], [w_down[layer]], [w_out[layer], w_branch_b[layer]])
    outs, lses, cast = [], [], []
    for g, (dil, rps) in enumerate(zip(DILATIONS, RESIDUES_PER_STEP)):
        o, lse, c = _attention(qs[g], ks[g], vs[g], batch, seq, dil, rps, side_casts[g])
        outs.append(o)
        lses.append(lse)
        cast += c
    w_up_b, w_down_b, w_out_b, w_b_b = cast
    out = _ffn(xf, ta, g1, outs, lses, w_b_b, w_out_b,
               ffn_norm_g[layer].reshape(1, d), w_up_b, conv_w[layer],
               conv_b[layer].reshape(1, D_FF), w_down_b,
               final_norm_g.reshape(1, d), batch, seq)
    return out.reshape(batch, seq, d)
```

```python
import functools

import jax
import jax.numpy as jnp
import numpy as np
from jax import lax
from jax.experimental import pallas as pl
from jax.experimental.pallas import tpu as pltpu

F32 = jnp.float32
BF16 = jnp.bfloat16

D_MODEL = 1024
EPS = 1e-6
GMLP_WIDTH = 768
GMLP_GROUPS = 4
GMLP_CHUNK = 128
HEAD_DIM = 64
HEADS_PER_GROUP = 4
GROUP_WIDTH = HEADS_PER_GROUP * HEAD_DIM
DILATIONS = (1, 4, 16)
RESIDUES_PER_STEP = (1, 4, 16)
BAND = 128
ATTN_BLOCK = 128
ATTN_GROUP = 4
ROPE_THETA = 500000.0
ROT_DIM = HEAD_DIM // 4
ROT_HALF = ROT_DIM // 2
PASS_DIM = HEAD_DIM - ROT_DIM
D_FF = 2816
FF_CHUNK = 256
N_FF_CHUNKS = D_FF // FF_CHUNK

OFF_Q = 2 * GMLP_WIDTH
OFF_K = OFF_Q + 3 * GROUP_WIDTH
OFF_V = OFF_K + 3 * GROUP_WIDTH
OFF_G0 = OFF_V + 3 * GROUP_WIDTH
OFF_G1 = OFF_G0 + D_MODEL
IN_WIDTH = OFF_G1 + D_MODEL

LANES = 128
PROJ_TILE = 512
PROJ_SUBTILE = 256
FFN_TILE = 512
FFN_SUBTILE = 256
VMEM_LIMIT_BYTES = 56 * 1024 * 1024


def _erf(x):
    return lax.erf(x)


def _gelu(x):
    return 0.5 * x * (1.0 + _erf(x * np.float32(np.sqrt(0.5))))


def _sigmoid(x):
    return 1.0 / (1.0 + jnp.exp(-x))


def _rms(x, g):
    ms = jnp.mean(x * x, axis=-1, keepdims=True)
    return x * lax.rsqrt(ms + EPS) * g


def _proj_kernel(x_ref, pos_ref, g_ref, w_ref, gg_ref, ws_ref, bias_ref, pa_ref, invf_ref, perm_ref,
                 q0_ref, q1_ref, q2_ref, k0_ref, k1_ref, k2_ref, v0_ref, v1_ref, v2_ref,
                 ta_ref, g1_ref, stage_ref, wqk_ref):
    tm = x_ref.shape[0]
    rows = PROJ_SUBTILE
    subs = range(tm // rows)

    @pl.when(pl.program_id(0) == 0)
    def _():
        for j in range(2 * len(DILATIONS)):
            cols = slice(j * GROUP_WIDTH, (j + 1) * GROUP_WIDTH)
            wqk_ref[:, cols] = jnp.dot(w_ref[:, OFF_Q + j * GROUP_WIDTH:OFF_Q + (j + 1) * GROUP_WIDTH],
                                       perm_ref[...], preferred_element_type=F32).astype(BF16)

    q_refs, k_refs, v_refs = (q0_ref, q1_ref, q2_ref), (k0_ref, k1_ref, k2_ref), (v0_ref, v1_ref, v2_ref)
    row = lax.broadcasted_iota(jnp.int32, (GMLP_CHUNK, GMLP_CHUNK), 0)
    col = lax.broadcasted_iota(jnp.int32, (GMLP_CHUNK, GMLP_CHUNK), 1)
    wm = [jnp.where(row >= col, ws_ref[g], 0.0).astype(BF16) for g in range(GMLP_GROUPS)]
    low_half = lax.broadcasted_iota(jnp.int32, (GMLP_CHUNK, LANES), 1) < (LANES // 2)

    def proj(h, lo, hi):
        return jnp.dot(h, w_ref[:, lo:hi], preferred_element_type=F32)

    def gating_inputs(sub):
        h = _rms(x_ref[sub * rows:(sub + 1) * rows], g_ref[...]).astype(BF16)
        vn = _rms(_gelu(proj(h, GMLP_WIDTH, 2 * GMLP_WIDTH)), gg_ref[...]).astype(BF16)
        u = _gelu(proj(h, 0, GMLP_WIDTH))
        return h, u, vn

    def store_by_residue(sub, halves, ref, g, slot):
        dil = DILATIONS[g]
        out_rows = slice(sub * rows // dil, (sub + 1) * rows // dil)
        if dil == 1:
            for s, half in enumerate(halves):
                ref[out_rows, s * LANES:(s + 1) * LANES] = half.astype(BF16)
            return
        for s, half in enumerate(halves):
            stage_ref[sub, slot, s] = half
        for r in range(dil):
            for s in range(2):
                lo = r * GROUP_WIDTH + s * LANES
                ref[out_rows, lo:lo + LANES] = stage_ref[
                    sub, slot, s, pl.ds(r, rows // dil, stride=dil), :].astype(BF16)

    def gates_and_qkv(sub, h):
        g0 = _sigmoid(proj(h, OFF_G0, OFF_G1))
        g1_ref[sub * rows:(sub + 1) * rows] = _sigmoid(proj(h, OFF_G1, IN_WIDTH)).astype(BF16)
        pos = pos_ref[:, sub * rows:(sub + 1) * rows].astype(F32)
        ang = jnp.concatenate([invf_ref[...] * pos[:, b * LANES:(b + 1) * LANES]
                               for b in range(rows // LANES)], axis=1)
        n_rot = 2 * HEADS_PER_GROUP
        cos = jnp.concatenate([jnp.cos(ang)] * n_rot + [jnp.ones((LANES - 8 * n_rot, rows), F32)], axis=0).T
        sin = jnp.concatenate([jnp.sin(ang)] * n_rot + [jnp.zeros((LANES - 8 * n_rot, rows), F32)], axis=0).T
        lane = lax.broadcasted_iota(jnp.int32, (rows, LANES), 1)
        sin_a = jnp.where(lane < 32, -sin, 0.0)
        sin_b = jnp.where(lane < 32, 0.0, sin)
        for off, refs, slot0 in ((OFF_Q, q_refs, 0), (OFF_K, k_refs, 2)):
            for g, ref in enumerate(refs):
                lo = off - OFF_Q + g * GROUP_WIDTH
                t = jnp.dot(h, wqk_ref[:, lo:lo + GROUP_WIDTH], preferred_element_type=F32)
                t0 = t[:, :LANES]
                t0 = t0 * cos + pltpu.roll(t0, LANES - 32, 1) * sin_a + pltpu.roll(t0, 32, 1) * sin_b
                store_by_residue(sub, (t0, t[:, LANES:]), ref, g, slot0 + g - 1)
        for g, ref in enumerate(v_refs):
            t = proj(h, OFF_V + g * GROUP_WIDTH, OFF_V + (g + 1) * GROUP_WIDTH)
            store_by_residue(sub, (t[:, :LANES], t[:, LANES:]), ref, g, 4 + g - 1)
        return g0

    def spatial_gating(sub, u, vn, g0):
        bias = bias_ref[...]
        ya_rows = []
        for c in range(rows // GMLP_CHUNK):
            vc = vn[c * GMLP_CHUNK:(c + 1) * GMLP_CHUNK]
            p0 = jnp.dot(wm[0], vc[:, 0:256], preferred_element_type=F32)
            p1 = jnp.dot(wm[1], vc[:, 128:384], preferred_element_type=F32)
            p2 = jnp.dot(wm[2], vc[:, 384:640], preferred_element_type=F32)
            p3 = jnp.dot(wm[3], vc[:, 512:768], preferred_element_type=F32)
            mixed = jnp.concatenate(
                [p0[:, :LANES], jnp.where(low_half, p0[:, LANES:], p1[:, :LANES]), p1[:, LANES:],
                 p2[:, :LANES], jnp.where(low_half, p2[:, LANES:], p3[:, :LANES]), p3[:, LANES:]],
                axis=1) + bias
            ya_rows.append(u[c * GMLP_CHUNK:(c + 1) * GMLP_CHUNK] * mixed)
        ya = jnp.concatenate(ya_rows, axis=0).astype(BF16)
        ta_ref[sub * rows:(sub + 1) * rows] = (
            g0 * jnp.dot(ya, pa_ref[...], preferred_element_type=F32)).astype(BF16)

    chains = [gating_inputs(sub) for sub in subs]
    gates = [gates_and_qkv(sub, chains[sub][0]) for sub in subs]
    for sub in subs:
        spatial_gating(sub, chains[sub][1], chains[sub][2], gates[sub])


def _resident(shape):
    nd = len(shape)
    return pl.BlockSpec(shape, lambda *_: (0,) * nd, pipeline_mode=pl.Buffered(1))


def _rotary_first_permutation():
    perm = np.zeros((GROUP_WIDTH, GROUP_WIDTH), np.float32)
    for h in range(HEADS_PER_GROUP):
        for dim in range(HEAD_DIM):
            if dim < ROT_HALF:
                dst = ROT_HALF * h + dim
            elif dim < ROT_DIM:
                dst = HEADS_PER_GROUP * ROT_HALF + ROT_HALF * h + (dim - ROT_HALF)
            else:
                dst = HEADS_PER_GROUP * ROT_DIM + PASS_DIM * h + (dim - ROT_DIM)
            perm[h * HEAD_DIM + dim, dst] = 1.0
    return jnp.asarray(perm, BF16)


def _projection(x2, pos2, mix_g, w_in, gmlp_g, w_s, bias_tbl, w_a, inv_freq):
    n = x2.shape[0]
    tm = PROJ_TILE
    row_spec = lambda w: pl.BlockSpec((tm, w), lambda i: (i, 0))
    grp_specs = [pl.BlockSpec((tm // d, d * GROUP_WIDTH), lambda i: (i, 0)) for d in DILATIONS] * 3
    grp_shapes = [jax.ShapeDtypeStruct((n // d, d * GROUP_WIDTH), BF16) for d in DILATIONS] * 3
    wide = jax.ShapeDtypeStruct((n, D_MODEL), BF16)
    return pl.pallas_call(
        _proj_kernel,
        grid=(n // tm,),
        in_specs=[row_spec(D_MODEL), pl.BlockSpec((None, 1, tm), lambda i: (i, 0, 0)),
                  _resident((1, D_MODEL)),
                  _resident((D_MODEL, IN_WIDTH)), _resident((1, GMLP_WIDTH)),
                  _resident((GMLP_GROUPS, GMLP_CHUNK, GMLP_CHUNK)),
                  _resident((GMLP_CHUNK, GMLP_WIDTH)), _resident((GMLP_WIDTH, D_MODEL)),
                  _resident((ROT_HALF, LANES)), _resident((GROUP_WIDTH, GROUP_WIDTH))],
        out_specs=grp_specs + [row_spec(D_MODEL)] * 2,
        out_shape=grp_shapes + [wide] * 2,
        scratch_shapes=[pltpu.VMEM((tm // PROJ_SUBTILE, 6, 2, PROJ_SUBTILE, LANES), F32),
                        pltpu.VMEM((D_MODEL, 2 * len(DILATIONS) * GROUP_WIDTH), BF16)],
        compiler_params=pltpu.CompilerParams(
            dimension_semantics=("arbitrary",), vmem_limit_bytes=VMEM_LIMIT_BYTES),
        name="projection",
    )(x2, pos2, mix_g, w_in, gmlp_g, w_s, bias_tbl, w_a, inv_freq, _rotary_first_permutation())


def _head_masks():
    lane = lax.broadcasted_iota(jnp.int32, (1, GROUP_WIDTH), 1)
    qk_head = jnp.where(lane < 2 * 32, (lane % 32) // ROT_HALF, (lane - 2 * 32) // PASS_DIM)
    v_head = lane // HEAD_DIM
    return qk_head, v_head


def _attn_q_masks():
    qk_head, _ = _head_masks()
    heads = jnp.arange(HEADS_PER_GROUP, dtype=jnp.int32)[:, None, None]
    mask = jnp.where(qk_head[None] == heads, 1.0, 0.0)
    return jnp.broadcast_to(mask, (HEADS_PER_GROUP, ATTN_BLOCK, GROUP_WIDTH)).astype(BF16)


def _attn_kernel(q_ref, k_ref, v_ref, qm_ref, *rest, n_res, n_blocks, group, n_casts):
    o_ref, lse_ref = rest[n_casts:n_casts + 2]
    for src, dst in zip(rest[:n_casts], rest[n_casts + 2:]):
        dst[...] = src[...].astype(BF16)
    nh = HEADS_PER_GROUP
    blk = ATTN_BLOCK
    _, v_head = _head_masks()
    qi = lax.broadcasted_iota(jnp.int32, (blk, 2 * blk), 0)
    ki = lax.broadcasted_iota(jnp.int32, (blk, 2 * blk), 1)
    band2 = jnp.where((ki >= qi) & (ki <= qi + BAND), 0.0, -jnp.inf).astype(F32)
    band1 = band2[:, blk:]
    band2 = jnp.concatenate([band2] * nh, axis=0)
    band1 = jnp.concatenate([band1] * nh, axis=0)
    low_half = lax.broadcasted_iota(jnp.int32, (blk, LANES), 1) < HEAD_DIM

    def per_head(x):
        halves = [jnp.where(low_half, x[2 * c * blk:(2 * c + 1) * blk], x[(2 * c + 1) * blk:(2 * c + 2) * blk])
                  for c in range(nh // 2)]
        return jnp.concatenate(halves, axis=1)

    def scores(cols, q_rows, kv_rows, band):
        qb = q_ref[q_rows, cols]
        qs = jnp.concatenate([qb * qm_ref[h] for h in range(nh)], axis=0)
        return lax.dot_general(qs, k_ref[kv_rows, cols], (((1,), (1,)), ((), ())),
                               preferred_element_type=F32) + band

    def softmax(s):
        m = jnp.max(s, axis=-1, keepdims=True)
        p = jnp.exp2(s - m)
        return p.astype(BF16), m, jnp.sum(p, axis=-1, keepdims=True)

    def finish(cols, q_rows, kv_rows, p, m, den):
        pv = jnp.dot(p, v_ref[kv_rows, cols], preferred_element_type=F32)
        out = pv[(nh - 1) * blk:]
        for h in range(nh - 2, -1, -1):
            out = jnp.where(v_head == h, pv[h * blk:(h + 1) * blk], out)
        den = per_head(jnp.broadcast_to(den, (nh * blk, LANES)))
        o_ref[q_rows, cols] = (out / den).astype(o_ref.dtype)
        m_wide = per_head(jnp.broadcast_to(m, (nh * blk, LANES)))
        lse_ref[q_rows, cols] = (m_wide + jnp.log2(den)) * np.float32(np.log(2.0))

    def block_group(cols, blocks):
        s = [scores(cols, *b) for b in blocks]
        pmd = [softmax(x) for x in s]
        for b, (p, m, den) in zip(blocks, pmd):
            finish(cols, b[0], b[1], p, m, den)

    def later_block(q0):
        return slice(q0, q0 + blk), slice(q0 - blk, q0 + blk), band2

    for r in range(n_res):
        cols = slice(r * GROUP_WIDTH, (r + 1) * GROUP_WIDTH)
        first = [(slice(0, blk), slice(0, blk), band1)]
        block_group(cols, first + [later_block(j * blk) for j in range(1, group)])
        for i in range(1, n_blocks // group):
            block_group(cols, [later_block(j * blk) for j in range(i * group, (i + 1) * group)])


def _attention(q, k, v, batch, seq, dil, res_per_step, casts):
    sub = seq // dil
    width = res_per_step * GROUP_WIDTH
    view = lambda t: t.reshape(batch, sub, dil * GROUP_WIDTH)
    spec = pl.BlockSpec((None, sub, width), lambda b, r: (b, 0, r))
    n_blocks = sub // ATTN_BLOCK
    qm = _attn_q_masks()
    cast_specs = [pl.BlockSpec((w.shape[0] // batch, w.shape[1]), lambda b, r: (b, 0)) for w in casts]
    kern = functools.partial(_attn_kernel, n_res=res_per_step, n_blocks=n_blocks,
                             group=min(ATTN_GROUP, n_blocks), n_casts=len(casts))
    o, lse, *cast = pl.pallas_call(
        kern,
        grid=(batch, dil // res_per_step),
        in_specs=[spec, spec, spec, _resident(qm.shape)] + cast_specs,
        out_specs=[spec, spec] + cast_specs,
        out_shape=[jax.ShapeDtypeStruct((batch, sub, dil * GROUP_WIDTH), BF16),
                   jax.ShapeDtypeStruct((batch, sub, dil * GROUP_WIDTH), F32)]
                  + [jax.ShapeDtypeStruct(w.shape, BF16) for w in casts],
        compiler_params=pltpu.CompilerParams(
            dimension_semantics=("arbitrary", "arbitrary"), vmem_limit_bytes=VMEM_LIMIT_BYTES),
        name=f"attention_d{dil}",
    )(view(q), view(k), view(v), qm, *casts)
    return o.reshape(batch * sub, dil * GROUP_WIDTH), lse.reshape(batch * sub, dil * GROUP_WIDTH), cast


def _ffn_kernel(x_ref, ta_ref, g1_ref, o0_ref, o1_ref, o2_ref, l0_ref, l1_ref, l2_ref,
                pb_ref, wo_ref, fg_ref, wup_ref, cw_ref, cb_ref, wd_ref, og_ref,
                out_ref, carry_ref, act_ref, stage_ref, mid_ref):
    tm = x_ref.shape[0]
    rows = FFN_SUBTILE
    subs = range(tm // rows)

    @pl.when(pl.program_id(1) == 0)
    def _():
        carry_ref[...] = jnp.zeros_like(carry_ref)

    def natural(sub, ref, g, slot):
        dil = DILATIONS[g]
        view_rows = slice(sub * rows // dil, (sub + 1) * rows // dil)
        if dil == 1:
            return ref[view_rows, :].astype(F32)
        for r in range(dil):
            for s in range(2):
                piece = ref[view_rows, r * GROUP_WIDTH + s * LANES:r * GROUP_WIDTH + (s + 1) * LANES].astype(F32)
                if dil == 4:
                    stage_ref[sub, slot, s, pl.ds(r, rows // 4, stride=4), :] = piece
                else:
                    r1, r2 = r % 4, r // 4
                    mid_ref[sub, slot // 2, s, pl.ds(r1 * (rows // 4) + r2, rows // 16, stride=4), :] = piece
        if dil == 16:
            for r1 in range(4):
                for s in range(2):
                    stage_ref[sub, slot, s, pl.ds(r1, rows // 4, stride=4), :] = (
                        mid_ref[sub, slot // 2, s, r1 * (rows // 4):(r1 + 1) * (rows // 4), :])
        return jnp.concatenate([stage_ref[sub, slot, 0], stage_ref[sub, slot, 1]], axis=1)

    def merged_branches(sub):
        tok = slice(sub * rows, (sub + 1) * rows)
        l0, l1, l2 = natural(sub, l0_ref, 0, 0), natural(sub, l1_ref, 1, 0), natural(sub, l2_ref, 2, 1)
        lmax = jnp.maximum(jnp.maximum(l0, l1), l2)
        e0, e1, e2 = jnp.exp(l0 - lmax), jnp.exp(l1 - lmax), jnp.exp(l2 - lmax)
        yb = (e0 * natural(sub, o0_ref, 0, 0) + e1 * natural(sub, o1_ref, 1, 2)
              + e2 * natural(sub, o2_ref, 2, 3)) / (e0 + e1 + e2)
        merged = ta_ref[tok, :].astype(F32) + g1_ref[tok, :].astype(F32) * jnp.dot(
            yb.astype(BF16), pb_ref[...], preferred_element_type=F32)
        return merged.astype(BF16)

    def out_proj(sub, merged):
        tok = slice(sub * rows, (sub + 1) * rows)
        x1 = x_ref[tok, :] + jnp.dot(merged, wo_ref[...], preferred_element_type=F32)
        return x1, _rms(x1, fg_ref[...]).astype(BF16)

    row8 = lax.broadcasted_iota(jnp.int32, (8, FF_CHUNK), 0)

    def up_chunk(sub, c, h2, prev):
        cols = slice(c * FF_CHUNK, (c + 1) * FF_CHUNK)
        a = jnp.dot(h2, wup_ref[:, cols], preferred_element_type=F32)
        val = jnp.dot(h2, wup_ref[:, D_FF + c * FF_CHUNK:D_FF + (c + 1) * FF_CHUNK],
                      preferred_element_type=F32)
        s1 = pltpu.roll(a, 1, 0)
        s2 = pltpu.roll(a, 2, 0)
        s1 = jnp.concatenate([jnp.where(row8 < 1, pltpu.roll(prev, 1, 0), s1[:8]), s1[8:]], axis=0)
        s2 = jnp.concatenate([jnp.where(row8 < 2, pltpu.roll(prev, 2, 0), s2[:8]), s2[8:]], axis=0)
        w = cw_ref[:, cols]
        y = s2 * w[0:1] + s1 * w[1:2] + a * w[2:3] + cb_ref[:, cols]
        act_ref[sub * rows:(sub + 1) * rows, cols] = (_gelu(y) * val).astype(BF16)
        return a[rows - 8:]

    merged = [merged_branches(sub) for sub in subs]
    xh = [out_proj(sub, merged[sub]) for sub in subs]
    for c in range(N_FF_CHUNKS):
        cols = slice(c * FF_CHUNK, (c + 1) * FF_CHUNK)
        prev = carry_ref[:, cols]
        for sub in subs:
            prev = up_chunk(sub, c, xh[sub][1], prev)
        carry_ref[:, cols] = prev
    for sub in subs:
        tok = slice(sub * rows, (sub + 1) * rows)
        x2 = xh[sub][0] + jnp.dot(act_ref[tok, :], wd_ref[...], preferred_element_type=F32)
        out_ref[tok, :] = _rms(x2, og_ref[...])


def _ffn(x, ta, g1, os_, ls_, w_b, w_out, ffn_g, w_up, conv_w, conv_b, w_down, final_g, batch, seq):
    tm = FFN_TILE
    tiles = seq // tm
    row_spec = lambda w: pl.BlockSpec((tm, w), lambda b, i: (b * tiles + i, 0))
    grp_specs = [pl.BlockSpec((tm // d, d * GROUP_WIDTH), lambda b, i: (b * tiles + i, 0))
                 for d in DILATIONS]
    return pl.pallas_call(
        _ffn_kernel,
        grid=(batch, tiles),
        in_specs=[row_spec(D_MODEL), row_spec(D_MODEL), row_spec(D_MODEL)]
                 + grp_specs * 2
                 + [_resident((GROUP_WIDTH, D_MODEL)), _resident((D_MODEL, D_MODEL)),
                    _resident((1, D_MODEL)), _resident((D_MODEL, 2 * D_FF)),
                    _resident((3, D_FF)), _resident((1, D_FF)), _resident((D_FF, D_MODEL)),
                    _resident((1, D_MODEL))],
        out_specs=row_spec(D_MODEL),
        out_shape=jax.ShapeDtypeStruct((batch * seq, D_MODEL), F32),
        scratch_shapes=[pltpu.VMEM((8, D_FF), F32), pltpu.VMEM((tm, D_FF), BF16),
                        pltpu.VMEM((tm // FFN_SUBTILE, 4, 2, FFN_SUBTILE, LANES), F32),
                        pltpu.VMEM((tm // FFN_SUBTILE, 2, 2, FFN_SUBTILE, LANES), F32)],
        compiler_params=pltpu.CompilerParams(
            dimension_semantics=("arbitrary", "arbitrary"), vmem_limit_bytes=VMEM_LIMIT_BYTES),
        name="mix_ffn",
    )(x, ta, g1, *os_, *ls_, w_b, w_out, ffn_g, w_up, conv_w, conv_b, w_down, final_g)


def kernel(x, positions, mix_norm_g, w_in, gmlp_norm_g, w_spatial, b_spatial, w_branch_a,
           w_branch_b, w_out, ffn_norm_g, w_up, conv_w, conv_b, w_down, final_norm_g):
    batch, seq, d = x.shape
    assert w_in.shape[0] == 1, "single-layer block: the final norm is fused into the FFN stage"
    layer = 0
    n = batch * seq
    inv_freq = ROPE_THETA ** (-jnp.arange(0, ROT_DIM, 2, dtype=F32) / ROT_DIM)
    inv_freq = jnp.broadcast_to(inv_freq[:, None], (ROT_HALF, LANES))
    pos2 = positions.reshape(n // PROJ_TILE, 1, PROJ_TILE)
    xf = x.reshape(n, d)
    col_scale = np.ones((IN_WIDTH,), np.float32)
    col_scale[OFF_Q:OFF_K] = HEAD_DIM ** -0.5
    col_scale[OFF_K:OFF_V] = np.log2(np.e)
    wi = (w_in[layer] * col_scale).astype(BF16)
    bias_tbl = jnp.repeat(b_spatial[layer].T, GMLP_WIDTH // GMLP_GROUPS, axis=1)
    res = _projection(xf, pos2, mix_norm_g[layer].reshape(1, d), wi,
                      gmlp_norm_g[layer].reshape(1, GMLP_WIDTH), w_spatial[layer], bias_tbl,
                      w_branch_a[layer].astype(BF16), inv_freq)
    qs, ks, vs, ta, g1 = res[0:3], res[3:6], res[6:9], res[9], res[10]
    side_casts = ([w_up[layer]], [w_down[layer]], [w_out[layer], w_branch_b[layer]])
    outs, lses, cast = [], [], []
    for g, (dil, rps) in enumerate(zip(DILATIONS, RESIDUES_PER_STEP)):
        o, lse, c = _attention(qs[g], ks[g], vs[g], batch, seq, dil, rps, side_casts[g])
        outs.append(o)
        lses.append(lse)
        cast += c
    w_up_b, w_down_b, w_out_b, w_b_b = cast
    out = _ffn(xf, ta, g1, outs, lses, w_b_b, w_out_b,
               ffn_norm_g[layer].reshape(1, d), w_up_b, conv_w[layer],
               conv_b[layer].reshape(1, D_FF), w_down_b,
               final_norm_g.reshape(1, d), batch, seq)
    return out.reshape(batch, seq, d)
```

```python
import functools

import jax
import jax.numpy as jnp
import numpy as np
from jax import lax
from jax.experimental import pallas as pl
from jax.experimental.pallas import tpu as pltpu

F32 = jnp.float32
BF16 = jnp.bfloat16

D_MODEL = 1024
EPS = 1e-6
GMLP_WIDTH = 768
GMLP_GROUPS = 4
GMLP_CHUNK = 128
HEAD_DIM = 64
HEADS_PER_GROUP = 4
GROUP_WIDTH = HEADS_PER_GROUP * HEAD_DIM
DILATIONS = (1, 4, 16)
RESIDUES_PER_STEP = (1, 4, 16)
BAND = 128
ATTN_BLOCK = 128
ATTN_GROUP = 4
ROPE_THETA = 500000.0
ROT_DIM = HEAD_DIM // 4
ROT_HALF = ROT_DIM // 2
PASS_DIM = HEAD_DIM - ROT_DIM
D_FF = 2816
FF_CHUNK = 256
N_FF_CHUNKS = D_FF // FF_CHUNK

OFF_Q = 2 * GMLP_WIDTH
OFF_K = OFF_Q + 3 * GROUP_WIDTH
OFF_V = OFF_K + 3 * GROUP_WIDTH
OFF_G0 = OFF_V + 3 * GROUP_WIDTH
OFF_G1 = OFF_G0 + D_MODEL
IN_WIDTH = OFF_G1 + D_MODEL

LANES = 128
PROJ_TILE = 512
PROJ_SUBTILE = 256
FFN_TILE = 512
FFN_SUBTILE = 256
VMEM_LIMIT_BYTES = 56 * 1024 * 1024


def _erf(x):
    return lax.erf(x)


def _gelu(x):
    return 0.5 * x * (1.0 + _erf(x * np.float32(np.sqrt(0.5))))


def _sigmoid(x):
    return 1.0 / (1.0 + jnp.exp(-x))


def _rms(x, g):
    ms = jnp.mean(x * x, axis=-1, keepdims=True)
    return x * lax.rsqrt(ms + EPS) * g


def _proj_kernel(x_ref, pos_ref, g_ref, w_ref, gg_ref, ws_ref, bias_ref, pa_ref, invf_ref, perm_ref,
                 q0_ref, q1_ref, q2_ref, k0_ref, k1_ref, k2_ref, v0_ref, v1_ref, v2_ref,
                 ta_ref, g1_ref, stage_ref, wqk_ref, mid_ref):
    tm = x_ref.shape[0]
    rows = PROJ_SUBTILE
    subs = range(tm // rows)

    @pl.when(pl.program_id(0) == 0)
    def _():
        for j in range(2 * len(DILATIONS)):
            cols = slice(j * GROUP_WIDTH, (j + 1) * GROUP_WIDTH)
            wqk_ref[:, cols] = jnp.dot(w_ref[:, OFF_Q + j * GROUP_WIDTH:OFF_Q + (j + 1) * GROUP_WIDTH],
                                       perm_ref[...], preferred_element_type=F32).astype(BF16)

    q_refs, k_refs, v_refs = (q0_ref, q1_ref, q2_ref), (k0_ref, k1_ref, k2_ref), (v0_ref, v1_ref, v2_ref)
    row = lax.broadcasted_iota(jnp.int32, (GMLP_CHUNK, GMLP_CHUNK), 0)
    col = lax.broadcasted_iota(jnp.int32, (GMLP_CHUNK, GMLP_CHUNK), 1)
    wm = [jnp.where(row >= col, ws_ref[g], 0.0).astype(BF16) for g in range(GMLP_GROUPS)]
    low_half = lax.broadcasted_iota(jnp.int32, (GMLP_CHUNK, LANES), 1) < (LANES // 2)

    def proj(h, lo, hi):
        return jnp.dot(h, w_ref[:, lo:hi], preferred_element_type=F32)

    def gating_inputs(sub):
        h = _rms(x_ref[sub * rows:(sub + 1) * rows], g_ref[...]).astype(BF16)
        vn = _rms(_gelu(proj(h, GMLP_WIDTH, 2 * GMLP_WIDTH)), gg_ref[...]).astype(BF16)
        u = _gelu(proj(h, 0, GMLP_WIDTH))
        return h, u, vn

    def store_by_residue(sub, halves, ref, g, slot):
        dil = DILATIONS[g]
        out_rows = slice(sub * rows // dil, (sub + 1) * rows // dil)
        if dil == 1:
            for s, half in enumerate(halves):
                ref[out_rows, s * LANES:(s + 1) * LANES] = half.astype(BF16)
            return
        for s, half in enumerate(halves):
            stage_ref[sub, slot, s] = half
        if dil == 16:
            for r1 in range(4):
                for s in range(2):
                    mid_ref[sub, slot // 2, s, r1 * (rows // 4):(r1 + 1) * (rows // 4), :] = stage_ref[
                        sub, slot, s, pl.ds(r1, rows // 4, stride=4), :]
        for r in range(dil):
            for s in range(2):
                lo = r * GROUP_WIDTH + s * LANES
                if dil == 4:
                    piece = stage_ref[sub, slot, s, pl.ds(r, rows // 4, stride=4), :]
                else:
                    piece = mid_ref[sub, slot // 2, s,
                                    pl.ds((r % 4) * (rows // 4) + r // 4, rows // 16, stride=4), :]
                ref[out_rows, lo:lo + LANES] = piece.astype(BF16)

    def gates_and_qkv(sub, h):
        g0 = _sigmoid(proj(h, OFF_G0, OFF_G1))
        g1_ref[sub * rows:(sub + 1) * rows] = _sigmoid(proj(h, OFF_G1, IN_WIDTH)).astype(BF16)
        pos = pos_ref[:, sub * rows:(sub + 1) * rows].astype(F32)
        ang = jnp.concatenate([invf_ref[...] * pos[:, b * LANES:(b + 1) * LANES]
                               for b in range(rows // LANES)], axis=1)
        n_rot = 2 * HEADS_PER_GROUP
        cos = jnp.concatenate([jnp.cos(ang)] * n_rot + [jnp.ones((LANES - 8 * n_rot, rows), F32)], axis=0).T
        sin = jnp.concatenate([jnp.sin(ang)] * n_rot + [jnp.zeros((LANES - 8 * n_rot, rows), F32)], axis=0).T
        lane = lax.broadcasted_iota(jnp.int32, (rows, LANES), 1)
        sin_a = jnp.where(lane < 32, -sin, 0.0)
        sin_b = jnp.where(lane < 32, 0.0, sin)
        for off, refs, slot0 in ((OFF_Q, q_refs, 0), (OFF_K, k_refs, 2)):
            for g, ref in enumerate(refs):
                lo = off - OFF_Q + g * GROUP_WIDTH
                t = jnp.dot(h, wqk_ref[:, lo:lo + GROUP_WIDTH], preferred_element_type=F32)
                t0 = t[:, :LANES]
                t0 = t0 * cos + pltpu.roll(t0, LANES - 32, 1) * sin_a + pltpu.roll(t0, 32, 1) * sin_b
                store_by_residue(sub, (t0, t[:, LANES:]), ref, g, slot0 + g - 1)
        for g, ref in enumerate(v_refs):
            t = proj(h, OFF_V + g * GROUP_WIDTH, OFF_V + (g + 1) * GROUP_WIDTH)
            store_by_residue(sub, (t[:, :LANES], t[:, LANES:]), ref, g, 4 + g - 1)
        return g0

    def spatial_gating(sub, u, vn, g0):
        bias = bias_ref[...]
        ya_rows = []
        for c in range(rows // GMLP_CHUNK):
            vc = vn[c * GMLP_CHUNK:(c + 1) * GMLP_CHUNK]
            p0 = jnp.dot(wm[0], vc[:, 0:256], preferred_element_type=F32)
            p1 = jnp.dot(wm[1], vc[:, 128:384], preferred_element_type=F32)
            p2 = jnp.dot(wm[2], vc[:, 384:640], preferred_element_type=F32)
            p3 = jnp.dot(wm[3], vc[:, 512:768], preferred_element_type=F32)
            mixed = jnp.concatenate(
                [p0[:, :LANES], jnp.where(low_half, p0[:, LANES:], p1[:, :LANES]), p1[:, LANES:],
                 p2[:, :LANES], jnp.where(low_half, p2[:, LANES:], p3[:, :LANES]), p3[:, LANES:]],
                axis=1) + bias
            ya_rows.append(u[c * GMLP_CHUNK:(c + 1) * GMLP_CHUNK] * mixed)
        ya = jnp.concatenate(ya_rows, axis=0).astype(BF16)
        ta_ref[sub * rows:(sub + 1) * rows] = (
            g0 * jnp.dot(ya, pa_ref[...], preferred_element_type=F32)).astype(BF16)

    chains = [gating_inputs(sub) for sub in subs]
    gates = [gates_and_qkv(sub, chains[sub][0]) for sub in subs]
    for sub in subs:
        spatial_gating(sub, chains[sub][1], chains[sub][2], gates[sub])


def _resident(shape):
    nd = len(shape)
    return pl.BlockSpec(shape, lambda *_: (0,) * nd, pipeline_mode=pl.Buffered(1))


def _rotary_first_permutation():
    perm = np.zeros((GROUP_WIDTH, GROUP_WIDTH), np.float32)
    for h in range(HEADS_PER_GROUP):
        for dim in range(HEAD_DIM):
            if dim < ROT_HALF:
                dst = ROT_HALF * h + dim
            elif dim < ROT_DIM:
                dst = HEADS_PER_GROUP * ROT_HALF + ROT_HALF * h + (dim - ROT_HALF)
            else:
                dst = HEADS_PER_GROUP * ROT_DIM + PASS_DIM * h + (dim - ROT_DIM)
            perm[h * HEAD_DIM + dim, dst] = 1.0
    return jnp.asarray(perm, BF16)


def _projection(x2, pos2, mix_g, w_in, gmlp_g, w_s, bias_tbl, w_a, inv_freq):
    n = x2.shape[0]
    tm = PROJ_TILE
    row_spec = lambda w: pl.BlockSpec((tm, w), lambda i: (i, 0))
    grp_specs = [pl.BlockSpec((tm // d, d * GROUP_WIDTH), lambda i: (i, 0)) for d in DILATIONS] * 3
    grp_shapes = [jax.ShapeDtypeStruct((n // d, d * GROUP_WIDTH), BF16) for d in DILATIONS] * 3
    wide = jax.ShapeDtypeStruct((n, D_MODEL), BF16)
    return pl.pallas_call(
        _proj_kernel,
        grid=(n // tm,),
        in_specs=[row_spec(D_MODEL), pl.BlockSpec((None, 1, tm), lambda i: (i, 0, 0)),
                  _resident((1, D_MODEL)),
                  _resident((D_MODEL, IN_WIDTH)), _resident((1, GMLP_WIDTH)),
                  _resident((GMLP_GROUPS, GMLP_CHUNK, GMLP_CHUNK)),
                  _resident((GMLP_CHUNK, GMLP_WIDTH)), _resident((GMLP_WIDTH, D_MODEL)),
                  _resident((ROT_HALF, LANES)), _resident((GROUP_WIDTH, GROUP_WIDTH))],
        out_specs=grp_specs + [row_spec(D_MODEL)] * 2,
        out_shape=grp_shapes + [wide] * 2,
        scratch_shapes=[pltpu.VMEM((tm // PROJ_SUBTILE, 6, 2, PROJ_SUBTILE, LANES), F32),
                        pltpu.VMEM((D_MODEL, 2 * len(DILATIONS) * GROUP_WIDTH), BF16),
                        pltpu.VMEM((tm // PROJ_SUBTILE, 3, 2, PROJ_SUBTILE, LANES), F32)],
        compiler_params=pltpu.CompilerParams(
            dimension_semantics=("arbitrary",), vmem_limit_bytes=VMEM_LIMIT_BYTES),
        name="projection",
    )(x2, pos2, mix_g, w_in, gmlp_g, w_s, bias_tbl, w_a, inv_freq, _rotary_first_permutation())


def _head_masks():
    lane = lax.broadcasted_iota(jnp.int32, (1, GROUP_WIDTH), 1)
    qk_head = jnp.where(lane < 2 * 32, (lane % 32) // ROT_HALF, (lane - 2 * 32) // PASS_DIM)
    v_head = lane // HEAD_DIM
    return qk_head, v_head


def _attn_q_masks():
    qk_head, _ = _head_masks()
    heads = jnp.arange(HEADS_PER_GROUP, dtype=jnp.int32)[:, None, None]
    mask = jnp.where(qk_head[None] == heads, 1.0, 0.0)
    return jnp.broadcast_to(mask, (HEADS_PER_GROUP, ATTN_BLOCK, GROUP_WIDTH)).astype(BF16)


def _attn_kernel(q_ref, k_ref, v_ref, qm_ref, *rest, n_res, n_blocks, group, n_casts):
    o_ref, lse_ref = rest[n_casts:n_casts + 2]
    for src, dst in zip(rest[:n_casts], rest[n_casts + 2:]):
        dst[...] = src[...].astype(BF16)
    nh = HEADS_PER_GROUP
    blk = ATTN_BLOCK
    _, v_head = _head_masks()
    qi = lax.broadcasted_iota(jnp.int32, (blk, 2 * blk), 0)
    ki = lax.broadcasted_iota(jnp.int32, (blk, 2 * blk), 1)
    band2 = jnp.where((ki >= qi) & (ki <= qi + BAND), 0.0, -jnp.inf).astype(F32)
    band1 = band2[:, blk:]
    band2 = jnp.concatenate([band2] * nh, axis=0)
    band1 = jnp.concatenate([band1] * nh, axis=0)
    low_half = lax.broadcasted_iota(jnp.int32, (blk, LANES), 1) < HEAD_DIM

    def per_head(x):
        halves = [jnp.where(low_half, x[2 * c * blk:(2 * c + 1) * blk], x[(2 * c + 1) * blk:(2 * c + 2) * blk])
                  for c in range(nh // 2)]
        return jnp.concatenate(halves, axis=1)

    def scores(cols, q_rows, kv_rows, band):
        qb = q_ref[q_rows, cols]
        qs = jnp.concatenate([qb * qm_ref[h] for h in range(nh)], axis=0)
        return lax.dot_general(qs, k_ref[kv_rows, cols], (((1,), (1,)), ((), ())),
                               preferred_element_type=F32) + band

    def softmax(s):
        m = jnp.max(s, axis=-1, keepdims=True)
        p = jnp.exp2(s - m)
        return p.astype(BF16), m, jnp.sum(p, axis=-1, keepdims=True)

    def finish(cols, q_rows, kv_rows, p, m, den):
        pv = jnp.dot(p, v_ref[kv_rows, cols], preferred_element_type=F32)
        out = pv[(nh - 1) * blk:]
        for h in range(nh - 2, -1, -1):
            out = jnp.where(v_head == h, pv[h * blk:(h + 1) * blk], out)
        den = per_head(jnp.broadcast_to(den, (nh * blk, LANES)))
        o_ref[q_rows, cols] = (out / den).astype(o_ref.dtype)
        m_wide = per_head(jnp.broadcast_to(m, (nh * blk, LANES)))
        lse_ref[q_rows, cols] = (m_wide + jnp.log2(den)) * np.float32(np.log(2.0))

    def block_group(cols, blocks):
        s = [scores(cols, *b) for b in blocks]
        pmd = [softmax(x) for x in s]
        for b, (p, m, den) in zip(blocks, pmd):
            finish(cols, b[0], b[1], p, m, den)

    def later_block(q0):
        return slice(q0, q0 + blk), slice(q0 - blk, q0 + blk), band2

    for r in range(n_res):
        cols = slice(r * GROUP_WIDTH, (r + 1) * GROUP_WIDTH)
        first = [(slice(0, blk), slice(0, blk), band1)]
        block_group(cols, first + [later_block(j * blk) for j in range(1, group)])
        for i in range(1, n_blocks // group):
            block_group(cols, [later_block(j * blk) for j in range(i * group, (i + 1) * group)])


def _attention(q, k, v, batch, seq, dil, res_per_step, casts):
    sub = seq // dil
    width = res_per_step * GROUP_WIDTH
    view = lambda t: t.reshape(batch, sub, dil * GROUP_WIDTH)
    spec = pl.BlockSpec((None, sub, width), lambda b, r: (b, 0, r))
    n_blocks = sub // ATTN_BLOCK
    qm = _attn_q_masks()
    cast_specs = [pl.BlockSpec((w.shape[0] // batch, w.shape[1]), lambda b, r: (b, 0)) for w in casts]
    kern = functools.partial(_attn_kernel, n_res=res_per_step, n_blocks=n_blocks,
                             group=min(ATTN_GROUP, n_blocks), n_casts=len(casts))
    o, lse, *cast = pl.pallas_call(
        kern,
        grid=(batch, dil // res_per_step),
        in_specs=[spec, spec, spec, _resident(qm.shape)] + cast_specs,
        out_specs=[spec, spec] + cast_specs,
        out_shape=[jax.ShapeDtypeStruct((batch, sub, dil * GROUP_WIDTH), BF16),
                   jax.ShapeDtypeStruct((batch, sub, dil * GROUP_WIDTH), F32)]
                  + [jax.ShapeDtypeStruct(w.shape, BF16) for w in casts],
        compiler_params=pltpu.CompilerParams(
            dimension_semantics=("arbitrary", "arbitrary"), vmem_limit_bytes=VMEM_LIMIT_BYTES),
        name=f"attention_d{dil}",
    )(view(q), view(k), view(v), qm, *casts)
    return o.reshape(batch * sub, dil * GROUP_WIDTH), lse.reshape(batch * sub, dil * GROUP_WIDTH), cast


def _ffn_kernel(x_ref, ta_ref, g1_ref, o0_ref, o1_ref, o2_ref, l0_ref, l1_ref, l2_ref,
                pb_ref, wo_ref, fg_ref, wup_ref, cw_ref, cb_ref, wd_ref, og_ref,
                out_ref, carry_ref, act_ref, stage_ref, mid_ref):
    tm = x_ref.shape[0]
    rows = FFN_SUBTILE
    subs = range(tm // rows)

    @pl.when(pl.program_id(1) == 0)
    def _():
        carry_ref[...] = jnp.zeros_like(carry_ref)

    def natural(sub, ref, g, slot):
        dil = DILATIONS[g]
        view_rows = slice(sub * rows // dil, (sub + 1) * rows // dil)
        if dil == 1:
            return ref[view_rows, :].astype(F32)
        for r in range(dil):
            for s in range(2):
                piece = ref[view_rows, r * GROUP_WIDTH + s * LANES:r * GROUP_WIDTH + (s + 1) * LANES].astype(F32)
                if dil == 4:
                    stage_ref[sub, slot, s, pl.ds(r, rows // 4, stride=4), :] = piece
                else:
                    r1, r2 = r % 4, r // 4
                    mid_ref[sub, slot // 2, s, pl.ds(r1 * (rows // 4) + r2, rows // 16, stride=4), :] = piece
        if dil == 16:
            for r1 in range(4):
                for s in range(2):
                    stage_ref[sub, slot, s, pl.ds(r1, rows // 4, stride=4), :] = (
                        mid_ref[sub, slot // 2, s, r1 * (rows // 4):(r1 + 1) * (rows // 4), :])
        return jnp.concatenate([stage_ref[sub, slot, 0], stage_ref[sub, slot, 1]], axis=1)

    def merged_branches(sub):
        tok = slice(sub * rows, (sub + 1) * rows)
        l0, l1, l2 = natural(sub, l0_ref, 0, 0), natural(sub, l1_ref, 1, 0), natural(sub, l2_ref, 2, 1)
        lmax = jnp.maximum(jnp.maximum(l0, l1), l2)
        e0, e1, e2 = jnp.exp(l0 - lmax), jnp.exp(l1 - lmax), jnp.exp(l2 - lmax)
        yb = (e0 * natural(sub, o0_ref, 0, 0) + e1 * natural(sub, o1_ref, 1, 2)
              + e2 * natural(sub, o2_ref, 2, 3)) / (e0 + e1 + e2)
        merged = ta_ref[tok, :].astype(F32) + g1_ref[tok, :].astype(F32) * jnp.dot(
            yb.astype(BF16), pb_ref[...], preferred_element_type=F32)
        return merged.astype(BF16)

    def out_proj(sub, merged):
        tok = slice(sub * rows, (sub + 1) * rows)
        x1 = x_ref[tok, :] + jnp.dot(merged, wo_ref[...], preferred_element_type=F32)
        return x1, _rms(x1, fg_ref[...]).astype(BF16)

    row8 = lax.broadcasted_iota(jnp.int32, (8, FF_CHUNK), 0)

    def up_chunk(sub, c, h2, prev):
        cols = slice(c * FF_CHUNK, (c + 1) * FF_CHUNK)
        a = jnp.dot(h2, wup_ref[:, cols], preferred_element_type=F32)
        val = jnp.dot(h2, wup_ref[:, D_FF + c * FF_CHUNK:D_FF + (c + 1) * FF_CHUNK],
                      preferred_element_type=F32)
        s1 = pltpu.roll(a, 1, 0)
        s2 = pltpu.roll(a, 2, 0)
        s1 = jnp.concatenate([jnp.where(row8 < 1, pltpu.roll(prev, 1, 0), s1[:8]), s1[8:]], axis=0)
        s2 = jnp.concatenate([jnp.where(row8 < 2, pltpu.roll(prev, 2, 0), s2[:8]), s2[8:]], axis=0)
        w = cw_ref[:, cols]
        y = s2 * w[0:1] + s1 * w[1:2] + a * w[2:3] + cb_ref[:, cols]
        act_ref[sub * rows:(sub + 1) * rows, cols] = (_gelu(y) * val).astype(BF16)
        return a[rows - 8:]

    merged = [merged_branches(sub) for sub in subs]
    xh = [out_proj(sub, merged[sub]) for sub in subs]
    for c in range(N_FF_CHUNKS):
        cols = slice(c * FF_CHUNK, (c + 1) * FF_CHUNK)
        prev = carry_ref[:, cols]
        for sub in subs:
            prev = up_chunk(sub, c, xh[sub][1], prev)
        carry_ref[:, cols] = prev
    for sub in subs:
        tok = slice(sub * rows, (sub + 1) * rows)
        x2 = xh[sub][0] + jnp.dot(act_ref[tok, :], wd_ref[...], preferred_element_type=F32)
        out_ref[tok, :] = _rms(x2, og_ref[...])


def _ffn(x, ta, g1, os_, ls_, w_b, w_out, ffn_g, w_up, conv_w, conv_b, w_down, final_g, batch, seq):
    tm = FFN_TILE
    tiles = seq // tm
    row_spec = lambda w: pl.BlockSpec((tm, w), lambda b, i: (b * tiles + i, 0))
    grp_specs = [pl.BlockSpec((tm // d, d * GROUP_WIDTH), lambda b, i: (b * tiles + i, 0))
                 for d in DILATIONS]
    return pl.pallas_call(
        _ffn_kernel,
        grid=(batch, tiles),
        in_specs=[row_spec(D_MODEL), row_spec(D_MODEL), row_spec(D_MODEL)]
                 + grp_specs * 2
                 + [_resident((GROUP_WIDTH, D_MODEL)), _resident((D_MODEL, D_MODEL)),
                    _resident((1, D_MODEL)), _resident((D_MODEL, 2 * D_FF)),
                    _resident((3, D_FF)), _resident((1, D_FF)), _resident((D_FF, D_MODEL)),
                    _resident((1, D_MODEL))],
        out_specs=row_spec(D_MODEL),
        out_shape=jax.ShapeDtypeStruct((batch * seq, D_MODEL), F32),
        scratch_shapes=[pltpu.VMEM((8, D_FF), F32), pltpu.VMEM((tm, D_FF), BF16),
                        pltpu.VMEM((tm // FFN_SUBTILE, 4, 2, FFN_SUBTILE, LANES), F32),
                        pltpu.VMEM((tm // FFN_SUBTILE, 2, 2, FFN_SUBTILE, LANES), F32)],
        compiler_params=pltpu.CompilerParams(
            dimension_semantics=("arbitrary", "arbitrary"), vmem_limit_bytes=VMEM_LIMIT_BYTES),
        name="mix_ffn",
    )(x, ta, g1, *os_, *ls_, w_b, w_out, ffn_g, w_up, conv_w, conv_b, w_down, final_g)


def kernel(x, positions, mix_norm_g, w_in, gmlp_norm_g, w_spatial, b_spatial, w_branch_a,
           w_branch_b, w_out, ffn_norm_g, w_up, conv_w, conv_b, w_down, final_norm_g):
    batch, seq, d = x.shape
    assert w_in.shape[0] == 1, "single-layer block: the final norm is fused into the FFN stage"
    layer = 0
    n = batch * seq
    inv_freq = ROPE_THETA ** (-jnp.arange(0, ROT_DIM, 2, dtype=F32) / ROT_DIM)
    inv_freq = jnp.broadcast_to(inv_freq[:, None], (ROT_HALF, LANES))
    pos2 = positions.reshape(n // PROJ_TILE, 1, PROJ_TILE)
    xf = x.reshape(n, d)
    col_scale = np.ones((IN_WIDTH,), np.float32)
    col_scale[OFF_Q:OFF_K] = HEAD_DIM ** -0.5
    col_scale[OFF_K:OFF_V] = np.log2(np.e)
    wi = (w_in[layer] * col_scale).astype(BF16)
    bias_tbl = jnp.repeat(b_spatial[layer].T, GMLP_WIDTH // GMLP_GROUPS, axis=1)
    res = _projection(xf, pos2, mix_norm_g[layer].reshape(1, d), wi,
                      gmlp_norm_g[layer].reshape(1, GMLP_WIDTH), w_spatial[layer], bias_tbl,
                      w_branch_a[layer].astype(BF16), inv_freq)
    qs, ks, vs, ta, g1 = res[0:3], res[3:6], res[6:9], res[9], res[10]
    side_casts = ([w_up[layer]], [w_down[layer]], [w_out[layer], w_branch_b[layer]])
    outs, lses, cast = [], [], []
    for g, (dil, rps) in enumerate(zip(DILATIONS, RESIDUES_PER_STEP)):
        o, lse, c = _attention(qs[g], ks[g], vs[g], batch, seq, dil, rps, side_casts[g])
        outs.append(o)
        lses.append(lse)
        cast += c
    w_up_b, w_down_b, w_out_b, w_b_b = cast
    out = _ffn(xf, ta, g1, outs, lses, w_b_b, w_out_b,
               ffn_norm_g[layer].reshape(1, d), w_up_b, conv_w[layer],
               conv_b[layer].reshape(1, D_FF), w_down_b,
               final_norm_g.reshape(1, d), batch, seq)
    return out.reshape(batch, seq, d)
```

```python
import functools

import jax
import jax.numpy as jnp
import numpy as np
from jax import lax
from jax.experimental import pallas as pl
from jax.experimental.pallas import tpu as pltpu

F32 = jnp.float32
BF16 = jnp.bfloat16

D_MODEL = 1024
EPS = 1e-6
GMLP_WIDTH = 768
GMLP_GROUPS = 4
GMLP_CHUNK = 128
HEAD_DIM = 64
HEADS_PER_GROUP = 4
GROUP_WIDTH = HEADS_PER_GROUP * HEAD_DIM
DILATIONS = (1, 4, 16)
RESIDUES_PER_STEP = (1, 4, 16)
BAND = 128
ATTN_BLOCK = 128
ATTN_GROUP = 4
ROPE_THETA = 500000.0
ROT_DIM = HEAD_DIM // 4
ROT_HALF = ROT_DIM // 2
PASS_DIM = HEAD_DIM - ROT_DIM
D_FF = 2816
FF_CHUNK = 256
N_FF_CHUNKS = D_FF // FF_CHUNK

OFF_Q = 2 * GMLP_WIDTH
OFF_K = OFF_Q + 3 * GROUP_WIDTH
OFF_V = OFF_K + 3 * GROUP_WIDTH
OFF_G0 = OFF_V + 3 * GROUP_WIDTH
OFF_G1 = OFF_G0 + D_MODEL
IN_WIDTH = OFF_G1 + D_MODEL

LANES = 128
PROJ_TILE = 512
PROJ_SUBTILE = 256
FFN_TILE = 512
FFN_SUBTILE = 256
FFN_SPLIT_CHUNKS = 1
VMEM_LIMIT_BYTES = 56 * 1024 * 1024


def _erf(x):
    return lax.erf(x)


def _gelu(x):
    return 0.5 * x * (1.0 + _erf(x * np.float32(np.sqrt(0.5))))


def _sigmoid(x):
    return 1.0 / (1.0 + jnp.exp(-x))


def _rms(x, g):
    ms = jnp.mean(x * x, axis=-1, keepdims=True)
    return x * lax.rsqrt(ms + EPS) * g


def _proj_kernel(x_ref, pos_ref, g_ref, w_ref, gg_ref, ws_ref, bias_ref, pa_ref, invf_ref, perm_ref,
                 q0_ref, q1_ref, q2_ref, k0_ref, k1_ref, k2_ref, v0_ref, v1_ref, v2_ref,
                 ta_ref, g1_ref, stage_ref, wqk_ref, mid_ref):
    tm = x_ref.shape[0]
    rows = PROJ_SUBTILE
    subs = range(tm // rows)

    @pl.when(pl.program_id(0) == 0)
    def _():
        for j in range(2 * len(DILATIONS)):
            cols = slice(j * GROUP_WIDTH, (j + 1) * GROUP_WIDTH)
            wqk_ref[:, cols] = jnp.dot(w_ref[:, OFF_Q + j * GROUP_WIDTH:OFF_Q + (j + 1) * GROUP_WIDTH],
                                       perm_ref[...], preferred_element_type=F32).astype(BF16)

    q_refs, k_refs, v_refs = (q0_ref, q1_ref, q2_ref), (k0_ref, k1_ref, k2_ref), (v0_ref, v1_ref, v2_ref)
    row = lax.broadcasted_iota(jnp.int32, (GMLP_CHUNK, GMLP_CHUNK), 0)
    col = lax.broadcasted_iota(jnp.int32, (GMLP_CHUNK, GMLP_CHUNK), 1)
    wm = [jnp.where(row >= col, ws_ref[g], 0.0).astype(BF16) for g in range(GMLP_GROUPS)]
    low_half = lax.broadcasted_iota(jnp.int32, (GMLP_CHUNK, LANES), 1) < (LANES // 2)

    def proj(h, lo, hi):
        return jnp.dot(h, w_ref[:, lo:hi], preferred_element_type=F32)

    def gating_inputs(sub):
        h = _rms(x_ref[sub * rows:(sub + 1) * rows], g_ref[...]).astype(BF16)
        vn = _rms(_gelu(proj(h, GMLP_WIDTH, 2 * GMLP_WIDTH)), gg_ref[...]).astype(BF16)
        u = _gelu(proj(h, 0, GMLP_WIDTH))
        return h, u, vn

    def store_by_residue(sub, halves, ref, g, slot):
        dil = DILATIONS[g]
        out_rows = slice(sub * rows // dil, (sub + 1) * rows // dil)
        if dil == 1:
            for s, half in enumerate(halves):
                ref[out_rows, s * LANES:(s + 1) * LANES] = half.astype(BF16)
            return
        for s, half in enumerate(halves):
            stage_ref[sub, slot, s] = half
        if dil == 16:
            for r1 in range(4):
                for s in range(2):
                    mid_ref[sub, slot // 2, s, r1 * (rows // 4):(r1 + 1) * (rows // 4), :] = stage_ref[
                        sub, slot, s, pl.ds(r1, rows // 4, stride=4), :]
        for r in range(dil):
            for s in range(2):
                lo = r * GROUP_WIDTH + s * LANES
                if dil == 4:
                    piece = stage_ref[sub, slot, s, pl.ds(r, rows // 4, stride=4), :]
                else:
                    piece = mid_ref[sub, slot // 2, s,
                                    pl.ds((r % 4) * (rows // 4) + r // 4, rows // 16, stride=4), :]
                ref[out_rows, lo:lo + LANES] = piece.astype(BF16)

    def gates_and_qkv(sub, h):
        g0 = _sigmoid(proj(h, OFF_G0, OFF_G1))
        g1_ref[sub * rows:(sub + 1) * rows] = _sigmoid(proj(h, OFF_G1, IN_WIDTH)).astype(BF16)
        pos = pos_ref[:, sub * rows:(sub + 1) * rows].astype(F32)
        ang = jnp.concatenate([invf_ref[...] * pos[:, b * LANES:(b + 1) * LANES]
                               for b in range(rows // LANES)], axis=1)
        n_rot = 2 * HEADS_PER_GROUP
        cos = jnp.concatenate([jnp.cos(ang)] * n_rot + [jnp.ones((LANES - 8 * n_rot, rows), F32)], axis=0).T
        sin = jnp.concatenate([jnp.sin(ang)] * n_rot + [jnp.zeros((LANES - 8 * n_rot, rows), F32)], axis=0).T
        lane = lax.broadcasted_iota(jnp.int32, (rows, LANES), 1)
        sin_a = jnp.where(lane < 32, -sin, 0.0)
        sin_b = jnp.where(lane < 32, 0.0, sin)
        for off, refs, slot0 in ((OFF_Q, q_refs, 0), (OFF_K, k_refs, 2)):
            for g, ref in enumerate(refs):
                lo = off - OFF_Q + g * GROUP_WIDTH
                t = jnp.dot(h, wqk_ref[:, lo:lo + GROUP_WIDTH], preferred_element_type=F32)
                t0 = t[:, :LANES]
                t0 = t0 * cos + pltpu.roll(t0, LANES - 32, 1) * sin_a + pltpu.roll(t0, 32, 1) * sin_b
                store_by_residue(sub, (t0, t[:, LANES:]), ref, g, slot0 + g - 1)
        for g, ref in enumerate(v_refs):
            t = proj(h, OFF_V + g * GROUP_WIDTH, OFF_V + (g + 1) * GROUP_WIDTH)
            store_by_residue(sub, (t[:, :LANES], t[:, LANES:]), ref, g, 4 + g - 1)
        return g0

    def spatial_gating(sub, u, vn, g0):
        bias = bias_ref[...]
        ya_rows = []
        for c in range(rows // GMLP_CHUNK):
            vc = vn[c * GMLP_CHUNK:(c + 1) * GMLP_CHUNK]
            p0 = jnp.dot(wm[0], vc[:, 0:256], preferred_element_type=F32)
            p1 = jnp.dot(wm[1], vc[:, 128:384], preferred_element_type=F32)
            p2 = jnp.dot(wm[2], vc[:, 384:640], preferred_element_type=F32)
            p3 = jnp.dot(wm[3], vc[:, 512:768], preferred_element_type=F32)
            mixed = jnp.concatenate(
                [p0[:, :LANES], jnp.where(low_half, p0[:, LANES:], p1[:, :LANES]), p1[:, LANES:],
                 p2[:, :LANES], jnp.where(low_half, p2[:, LANES:], p3[:, :LANES]), p3[:, LANES:]],
                axis=1) + bias
            ya_rows.append(u[c * GMLP_CHUNK:(c + 1) * GMLP_CHUNK] * mixed)
        ya = jnp.concatenate(ya_rows, axis=0).astype(BF16)
        ta_ref[sub * rows:(sub + 1) * rows] = (
            g0 * jnp.dot(ya, pa_ref[...], preferred_element_type=F32)).astype(BF16)

    chains = [gating_inputs(sub) for sub in subs]
    gates = [gates_and_qkv(sub, chains[sub][0]) for sub in subs]
    for sub in subs:
        spatial_gating(sub, chains[sub][1], chains[sub][2], gates[sub])


def _resident(shape):
    nd = len(shape)
    return pl.BlockSpec(shape, lambda *_: (0,) * nd, pipeline_mode=pl.Buffered(1))


def _rotary_first_permutation():
    perm = np.zeros((GROUP_WIDTH, GROUP_WIDTH), np.float32)
    for h in range(HEADS_PER_GROUP):
        for dim in range(HEAD_DIM):
            if dim < ROT_HALF:
                dst = ROT_HALF * h + dim
            elif dim < ROT_DIM:
                dst = HEADS_PER_GROUP * ROT_HALF + ROT_HALF * h + (dim - ROT_HALF)
            else:
                dst = HEADS_PER_GROUP * ROT_DIM + PASS_DIM * h + (dim - ROT_DIM)
            perm[h * HEAD_DIM + dim, dst] = 1.0
    return jnp.asarray(perm, BF16)


def _projection(x2, pos2, mix_g, w_in, gmlp_g, w_s, bias_tbl, w_a, inv_freq):
    n = x2.shape[0]
    tm = PROJ_TILE
    row_spec = lambda w: pl.BlockSpec((tm, w), lambda i: (i, 0))
    grp_specs = [pl.BlockSpec((tm // d, d * GROUP_WIDTH), lambda i: (i, 0)) for d in DILATIONS] * 3
    grp_shapes = [jax.ShapeDtypeStruct((n // d, d * GROUP_WIDTH), BF16) for d in DILATIONS] * 3
    wide = jax.ShapeDtypeStruct((n, D_MODEL), BF16)
    return pl.pallas_call(
        _proj_kernel,
        grid=(n // tm,),
        in_specs=[row_spec(D_MODEL), pl.BlockSpec((None, 1, tm), lambda i: (i, 0, 0)),
                  _resident((1, D_MODEL)),
                  _resident((D_MODEL, IN_WIDTH)), _resident((1, GMLP_WIDTH)),
                  _resident((GMLP_GROUPS, GMLP_CHUNK, GMLP_CHUNK)),
                  _resident((GMLP_CHUNK, GMLP_WIDTH)), _resident((GMLP_WIDTH, D_MODEL)),
                  _resident((ROT_HALF, LANES)), _resident((GROUP_WIDTH, GROUP_WIDTH))],
        out_specs=grp_specs + [row_spec(D_MODEL)] * 2,
        out_shape=grp_shapes + [wide] * 2,
        scratch_shapes=[pltpu.VMEM((tm // PROJ_SUBTILE, 6, 2, PROJ_SUBTILE, LANES), F32),
                        pltpu.VMEM((D_MODEL, 2 * len(DILATIONS) * GROUP_WIDTH), BF16),
                        pltpu.VMEM((tm // PROJ_SUBTILE, 3, 2, PROJ_SUBTILE, LANES), F32)],
        compiler_params=pltpu.CompilerParams(
            dimension_semantics=("arbitrary",), vmem_limit_bytes=VMEM_LIMIT_BYTES),
        name="projection",
    )(x2, pos2, mix_g, w_in, gmlp_g, w_s, bias_tbl, w_a, inv_freq, _rotary_first_permutation())


def _head_masks():
    lane = lax.broadcasted_iota(jnp.int32, (1, GROUP_WIDTH), 1)
    qk_head = jnp.where(lane < 2 * 32, (lane % 32) // ROT_HALF, (lane - 2 * 32) // PASS_DIM)
    v_head = lane // HEAD_DIM
    return qk_head, v_head


def _attn_q_masks():
    qk_head, _ = _head_masks()
    heads = jnp.arange(HEADS_PER_GROUP, dtype=jnp.int32)[:, None, None]
    mask = jnp.where(qk_head[None] == heads, 1.0, 0.0)
    return jnp.broadcast_to(mask, (HEADS_PER_GROUP, ATTN_BLOCK, GROUP_WIDTH)).astype(BF16)


def _attn_kernel(q_ref, k_ref, v_ref, qm_ref, *rest, n_res, n_blocks, group, n_casts):
    o_ref, lse_ref = rest[n_casts:n_casts + 2]
    for src, dst in zip(rest[:n_casts], rest[n_casts + 2:]):
        dst[...] = src[...].astype(BF16)
    nh = HEADS_PER_GROUP
    blk = ATTN_BLOCK
    _, v_head = _head_masks()
    qi = lax.broadcasted_iota(jnp.int32, (blk, 2 * blk), 0)
    ki = lax.broadcasted_iota(jnp.int32, (blk, 2 * blk), 1)
    band2 = jnp.where((ki >= qi) & (ki <= qi + BAND), 0.0, -jnp.inf).astype(F32)
    band1 = band2[:, blk:]
    band2 = jnp.concatenate([band2] * nh, axis=0)
    band1 = jnp.concatenate([band1] * nh, axis=0)
    low_half = lax.broadcasted_iota(jnp.int32, (blk, LANES), 1) < HEAD_DIM

    def per_head(x):
        halves = [jnp.where(low_half, x[2 * c * blk:(2 * c + 1) * blk], x[(2 * c + 1) * blk:(2 * c + 2) * blk])
                  for c in range(nh // 2)]
        return jnp.concatenate(halves, axis=1)

    def scores(cols, q_rows, kv_rows, band):
        qb = q_ref[q_rows, cols]
        qs = jnp.concatenate([qb * qm_ref[h] for h in range(nh)], axis=0)
        return lax.dot_general(qs, k_ref[kv_rows, cols], (((1,), (1,)), ((), ())),
                               preferred_element_type=F32) + band

    def softmax(s):
        m = jnp.max(s, axis=-1, keepdims=True)
        p = jnp.exp2(s - m)
        return p.astype(BF16), m, jnp.sum(p, axis=-1, keepdims=True)

    def finish(cols, q_rows, kv_rows, p, m, den):
        pv = jnp.dot(p, v_ref[kv_rows, cols], preferred_element_type=F32)
        out = pv[(nh - 1) * blk:]
        for h in range(nh - 2, -1, -1):
            out = jnp.where(v_head == h, pv[h * blk:(h + 1) * blk], out)
        den = per_head(jnp.broadcast_to(den, (nh * blk, LANES)))
        o_ref[q_rows, cols] = (out / den).astype(o_ref.dtype)
        m_wide = per_head(jnp.broadcast_to(m, (nh * blk, LANES)))
        lse_ref[q_rows, cols] = (m_wide + jnp.log2(den)) * np.float32(np.log(2.0))

    def block_group(cols, blocks):
        s = [scores(cols, *b) for b in blocks]
        pmd = [softmax(x) for x in s]
        for b, (p, m, den) in zip(blocks, pmd):
            finish(cols, b[0], b[1], p, m, den)

    def later_block(q0):
        return slice(q0, q0 + blk), slice(q0 - blk, q0 + blk), band2

    for r in range(n_res):
        cols = slice(r * GROUP_WIDTH, (r + 1) * GROUP_WIDTH)
        first = [(slice(0, blk), slice(0, blk), band1)]
        block_group(cols, first + [later_block(j * blk) for j in range(1, group)])
        for i in range(1, n_blocks // group):
            block_group(cols, [later_block(j * blk) for j in range(i * group, (i + 1) * group)])


def _attention(q, k, v, batch, seq, dil, res_per_step, casts):
    sub = seq // dil
    width = res_per_step * GROUP_WIDTH
    view = lambda t: t.reshape(batch, sub, dil * GROUP_WIDTH)
    spec = pl.BlockSpec((None, sub, width), lambda b, r: (b, 0, r))
    n_blocks = sub // ATTN_BLOCK
    qm = _attn_q_masks()
    cast_specs = [pl.BlockSpec((w.shape[0] // batch, w.shape[1]), lambda b, r: (b, 0)) for w in casts]
    kern = functools.partial(_attn_kernel, n_res=res_per_step, n_blocks=n_blocks,
                             group=min(ATTN_GROUP, n_blocks), n_casts=len(casts))
    o, lse, *cast = pl.pallas_call(
        kern,
        grid=(batch, dil // res_per_step),
        in_specs=[spec, spec, spec, _resident(qm.shape)] + cast_specs,
        out_specs=[spec, spec] + cast_specs,
        out_shape=[jax.ShapeDtypeStruct((batch, sub, dil * GROUP_WIDTH), BF16),
                   jax.ShapeDtypeStruct((batch, sub, dil * GROUP_WIDTH), F32)]
                  + [jax.ShapeDtypeStruct(w.shape, BF16) for w in casts],
        compiler_params=pltpu.CompilerParams(
            dimension_semantics=("arbitrary", "arbitrary"), vmem_limit_bytes=VMEM_LIMIT_BYTES),
        name=f"attention_d{dil}",
    )(view(q), view(k), view(v), qm, *casts)
    return o.reshape(batch * sub, dil * GROUP_WIDTH), lse.reshape(batch * sub, dil * GROUP_WIDTH), cast


def _ffn_kernel(x_ref, ta_ref, g1_ref, o0_ref, o1_ref, o2_ref, l0_ref, l1_ref, l2_ref,
                pb_ref, wo_ref, fg_ref, wup_ref, cw_ref, cb_ref, wd_ref, og_ref,
                out_ref, carry_ref, act_ref, stage_ref, mid_ref):
    tm = x_ref.shape[0]
    rows = FFN_SUBTILE
    subs = range(tm // rows)

    @pl.when(pl.program_id(1) == 0)
    def _():
        carry_ref[...] = jnp.zeros_like(carry_ref)

    def natural(sub, ref, g, slot):
        dil = DILATIONS[g]
        view_rows = slice(sub * rows // dil, (sub + 1) * rows // dil)
        if dil == 1:
            return ref[view_rows, :].astype(F32)
        for r in range(dil):
            for s in range(2):
                piece = ref[view_rows, r * GROUP_WIDTH + s * LANES:r * GROUP_WIDTH + (s + 1) * LANES].astype(F32)
                if dil == 4:
                    stage_ref[sub, slot, s, pl.ds(r, rows // 4, stride=4), :] = piece
                else:
                    r1, r2 = r % 4, r // 4
                    mid_ref[sub, slot // 2, s, pl.ds(r1 * (rows // 4) + r2, rows // 16, stride=4), :] = piece
        if dil == 16:
            for r1 in range(4):
                for s in range(2):
                    stage_ref[sub, slot, s, pl.ds(r1, rows // 4, stride=4), :] = (
                        mid_ref[sub, slot // 2, s, r1 * (rows // 4):(r1 + 1) * (rows // 4), :])
        return jnp.concatenate([stage_ref[sub, slot, 0], stage_ref[sub, slot, 1]], axis=1)

    def merged_branches(sub):
        tok = slice(sub * rows, (sub + 1) * rows)
        l0, l1, l2 = natural(sub, l0_ref, 0, 0), natural(sub, l1_ref, 1, 0), natural(sub, l2_ref, 2, 1)
        lmax = jnp.maximum(jnp.maximum(l0, l1), l2)
        e0, e1, e2 = jnp.exp(l0 - lmax), jnp.exp(l1 - lmax), jnp.exp(l2 - lmax)
        yb = (e0 * natural(sub, o0_ref, 0, 0) + e1 * natural(sub, o1_ref, 1, 2)
              + e2 * natural(sub, o2_ref, 2, 3)) / (e0 + e1 + e2)
        merged = ta_ref[tok, :].astype(F32) + g1_ref[tok, :].astype(F32) * jnp.dot(
            yb.astype(BF16), pb_ref[...], preferred_element_type=F32)
        return merged.astype(BF16)

    def out_proj(sub, merged):
        tok = slice(sub * rows, (sub + 1) * rows)
        x1 = x_ref[tok, :] + jnp.dot(merged, wo_ref[...], preferred_element_type=F32)
        return x1, _rms(x1, fg_ref[...]).astype(BF16)

    row8 = lax.broadcasted_iota(jnp.int32, (8, FF_CHUNK), 0)

    def up_chunk(c, r0, h2, prev):
        cols = slice(c * FF_CHUNK, (c + 1) * FF_CHUNK)
        a = jnp.dot(h2, wup_ref[:, cols], preferred_element_type=F32)
        val = jnp.dot(h2, wup_ref[:, D_FF + c * FF_CHUNK:D_FF + (c + 1) * FF_CHUNK],
                      preferred_element_type=F32)
        s1 = pltpu.roll(a, 1, 0)
        s2 = pltpu.roll(a, 2, 0)
        s1 = jnp.concatenate([jnp.where(row8 < 1, pltpu.roll(prev, 1, 0), s1[:8]), s1[8:]], axis=0)
        s2 = jnp.concatenate([jnp.where(row8 < 2, pltpu.roll(prev, 2, 0), s2[:8]), s2[8:]], axis=0)
        w = cw_ref[:, cols]
        y = s2 * w[0:1] + s1 * w[1:2] + a * w[2:3] + cb_ref[:, cols]
        act_ref[r0:r0 + h2.shape[0], cols] = (_gelu(y) * val).astype(BF16)
        return a[h2.shape[0] - 8:]

    merged = [merged_branches(sub) for sub in subs]
    xh = [out_proj(sub, merged[sub]) for sub in subs]
    h2 = jnp.concatenate([xh[sub][1] for sub in subs], axis=0)
    for c in range(N_FF_CHUNKS):
        cols = slice(c * FF_CHUNK, (c + 1) * FF_CHUNK)
        prev = carry_ref[:, cols]
        if c < FFN_SPLIT_CHUNKS:
            for sub in subs:
                prev = up_chunk(c, sub * rows, xh[sub][1], prev)
        else:
            prev = up_chunk(c, 0, h2, prev)
        carry_ref[:, cols] = prev
    for sub in subs:
        tok = slice(sub * rows, (sub + 1) * rows)
        x2 = xh[sub][0] + jnp.dot(act_ref[tok, :], wd_ref[...], preferred_element_type=F32)
        out_ref[tok, :] = _rms(x2, og_ref[...])


def _ffn(x, ta, g1, os_, ls_, w_b, w_out, ffn_g, w_up, conv_w, conv_b, w_down, final_g, batch, seq):
    tm = FFN_TILE
    tiles = seq // tm
    row_spec = lambda w: pl.BlockSpec((tm, w), lambda b, i: (b * tiles + i, 0))
    grp_specs = [pl.BlockSpec((tm // d, d * GROUP_WIDTH), lambda b, i: (b * tiles + i, 0))
                 for d in DILATIONS]
    return pl.pallas_call(
        _ffn_kernel,
        grid=(batch, tiles),
        in_specs=[row_spec(D_MODEL), row_spec(D_MODEL), row_spec(D_MODEL)]
                 + grp_specs * 2
                 + [_resident((GROUP_WIDTH, D_MODEL)), _resident((D_MODEL, D_MODEL)),
                    _resident((1, D_MODEL)), _resident((D_MODEL, 2 * D_FF)),
                    _resident((3, D_FF)), _resident((1, D_FF)), _resident((D_FF, D_MODEL)),
                    _resident((1, D_MODEL))],
        out_specs=row_spec(D_MODEL),
        out_shape=jax.ShapeDtypeStruct((batch * seq, D_MODEL), F32),
        scratch_shapes=[pltpu.VMEM((8, D_FF), F32), pltpu.VMEM((tm, D_FF), BF16),
                        pltpu.VMEM((tm // FFN_SUBTILE, 4, 2, FFN_SUBTILE, LANES), F32),
                        pltpu.VMEM((tm // FFN_SUBTILE, 2, 2, FFN_SUBTILE, LANES), F32)],
        compiler_params=pltpu.CompilerParams(
            dimension_semantics=("arbitrary", "arbitrary"), vmem_limit_bytes=VMEM_LIMIT_BYTES),
        name="mix_ffn",
    )(x, ta, g1, *os_, *ls_, w_b, w_out, ffn_g, w_up, conv_w, conv_b, w_down, final_g)


def kernel(x, positions, mix_norm_g, w_in, gmlp_norm_g, w_spatial, b_spatial, w_branch_a,
           w_branch_b, w_out, ffn_norm_g, w_up, conv_w, conv_b, w_down, final_norm_g):
    batch, seq, d = x.shape
    assert w_in.shape[0] == 1, "single-layer block: the final norm is fused into the FFN stage"
    layer = 0
    n = batch * seq
    inv_freq = ROPE_THETA ** (-jnp.arange(0, ROT_DIM, 2, dtype=F32) / ROT_DIM)
    inv_freq = jnp.broadcast_to(inv_freq[:, None], (ROT_HALF, LANES))
    pos2 = positions.reshape(n // PROJ_TILE, 1, PROJ_TILE)
    xf = x.reshape(n, d)
    col_scale = np.ones((IN_WIDTH,), np.float32)
    col_scale[OFF_Q:OFF_K] = HEAD_DIM ** -0.5
    col_scale[OFF_K:OFF_V] = np.log2(np.e)
    wi = (w_in[layer] * col_scale).astype(BF16)
    bias_tbl = jnp.repeat(b_spatial[layer].T, GMLP_WIDTH // GMLP_GROUPS, axis=1)
    res = _projection(xf, pos2, mix_norm_g[layer].reshape(1, d), wi,
                      gmlp_norm_g[layer].reshape(1, GMLP_WIDTH), w_spatial[layer], bias_tbl,
                      w_branch_a[layer].astype(BF16), inv_freq)
    qs, ks, vs, ta, g1 = res[0:3], res[3:6], res[6:9], res[9], res[10]
    side_casts = ([w_up[layer]], [w_down[layer]], [w_out[layer], w_branch_b[layer]])
    outs, lses, cast = [], [], []
    for g, (dil, rps) in enumerate(zip(DILATIONS, RESIDUES_PER_STEP)):
        o, lse, c = _attention(qs[g], ks[g], vs[g], batch, seq, dil, rps, side_casts[g])
        outs.append(o)
        lses.append(lse)
        cast += c
    w_up_b, w_down_b, w_out_b, w_b_b = cast
    out = _ffn(xf, ta, g1, outs, lses, w_b_b, w_out_b,
               ffn_norm_g[layer].reshape(1, d), w_up_b, conv_w[layer],
               conv_b[layer].reshape(1, D_FF), w_down_b,
               final_norm_g.reshape(1, d), batch, seq)
    return out.reshape(batch, seq, d)
```

```python
import functools

import jax
import jax.numpy as jnp
import numpy as np
from jax import lax
from jax.experimental import pallas as pl
from jax.experimental.pallas import tpu as pltpu

F32 = jnp.float32
BF16 = jnp.bfloat16

D_MODEL = 1024
EPS = 1e-6
GMLP_WIDTH = 768
GMLP_GROUPS = 4
GMLP_CHUNK = 128
HEAD_DIM = 64
HEADS_PER_GROUP = 4
GROUP_WIDTH = HEADS_PER_GROUP * HEAD_DIM
DILATIONS = (1, 4, 16)
RESIDUES_PER_STEP = (1, 4, 16)
BAND = 128
ATTN_BLOCK = 128
ATTN_GROUP = 4
ROPE_THETA = 500000.0
ROT_DIM = HEAD_DIM // 4
ROT_HALF = ROT_DIM // 2
PASS_DIM = HEAD_DIM - ROT_DIM
D_FF = 2816
FF_CHUNK = 256
N_FF_CHUNKS = D_FF // FF_CHUNK

OFF_Q = 2 * GMLP_WIDTH
OFF_K = OFF_Q + 3 * GROUP_WIDTH
OFF_V = OFF_K + 3 * GROUP_WIDTH
OFF_G0 = OFF_V + 3 * GROUP_WIDTH
OFF_G1 = OFF_G0 + D_MODEL
IN_WIDTH = OFF_G1 + D_MODEL

LANES = 128
PROJ_TILE = 512
PROJ_SUBTILE = 256
FFN_TILE = 512
FFN_SUBTILE = 256
VMEM_LIMIT_BYTES = 56 * 1024 * 1024


def _erf(x):
    return lax.erf(x)


def _gelu(x):
    return 0.5 * x * (1.0 + _erf(x * np.float32(np.sqrt(0.5))))


def _sigmoid(x):
    return 1.0 / (1.0 + jnp.exp(-x))


def _rms(x, g):
    ms = jnp.mean(x * x, axis=-1, keepdims=True)
    return x * lax.rsqrt(ms + EPS) * g


def _rms_deferred(x, g):
    ms = jnp.mean(x * x, axis=-1, keepdims=True)
    return (x * g).astype(BF16), lax.rsqrt(ms + EPS)


def _proj_kernel(x_ref, pos_ref, g_ref, w_ref, gg_ref, ws_ref, bias_ref, pa_ref, invf_ref, perm_ref,
                 q0_ref, q1_ref, q2_ref, k0_ref, k1_ref, k2_ref, v0_ref, v1_ref, v2_ref,
                 ta_ref, g1_ref, stage_ref, wqk_ref, mid_ref):
    tm = x_ref.shape[0]
    rows = PROJ_SUBTILE
    subs = range(tm // rows)

    @pl.when(pl.program_id(0) == 0)
    def _():
        for j in range(2 * len(DILATIONS)):
            cols = slice(j * GROUP_WIDTH, (j + 1) * GROUP_WIDTH)
            wqk_ref[:, cols] = jnp.dot(w_ref[:, OFF_Q + j * GROUP_WIDTH:OFF_Q + (j + 1) * GROUP_WIDTH],
                                       perm_ref[...], preferred_element_type=F32).astype(BF16)

    q_refs, k_refs, v_refs = (q0_ref, q1_ref, q2_ref), (k0_ref, k1_ref, k2_ref), (v0_ref, v1_ref, v2_ref)
    row = lax.broadcasted_iota(jnp.int32, (GMLP_CHUNK, GMLP_CHUNK), 0)
    col = lax.broadcasted_iota(jnp.int32, (GMLP_CHUNK, GMLP_CHUNK), 1)
    wm = [jnp.where(row >= col, ws_ref[g], 0.0).astype(BF16) for g in range(GMLP_GROUPS)]
    low_half = lax.broadcasted_iota(jnp.int32, (GMLP_CHUNK, LANES), 1) < (LANES // 2)

    def proj(h, lo, hi):
        xg, scale = h
        return jnp.dot(xg, w_ref[:, lo:hi], preferred_element_type=F32) * scale

    def gating_inputs(sub):
        h = _rms_deferred(x_ref[sub * rows:(sub + 1) * rows], g_ref[...])
        vn = _rms(_gelu(proj(h, GMLP_WIDTH, 2 * GMLP_WIDTH)), gg_ref[...]).astype(BF16)
        u = _gelu(proj(h, 0, GMLP_WIDTH))
        return h, u, vn

    def store_by_residue(sub, halves, ref, g, slot):
        dil = DILATIONS[g]
        out_rows = slice(sub * rows // dil, (sub + 1) * rows // dil)
        if dil == 1:
            for s, half in enumerate(halves):
                ref[out_rows, s * LANES:(s + 1) * LANES] = half.astype(BF16)
            return
        for s, half in enumerate(halves):
            stage_ref[sub, slot, s] = half
        if dil == 16:
            for r1 in range(4):
                for s in range(2):
                    mid_ref[sub, slot // 2, s, r1 * (rows // 4):(r1 + 1) * (rows // 4), :] = stage_ref[
                        sub, slot, s, pl.ds(r1, rows // 4, stride=4), :]
        for r in range(dil):
            for s in range(2):
                lo = r * GROUP_WIDTH + s * LANES
                if dil == 4:
                    piece = stage_ref[sub, slot, s, pl.ds(r, rows // 4, stride=4), :]
                else:
                    piece = mid_ref[sub, slot // 2, s,
                                    pl.ds((r % 4) * (rows // 4) + r // 4, rows // 16, stride=4), :]
                ref[out_rows, lo:lo + LANES] = piece.astype(BF16)

    def gates_and_qkv(sub, h):
        g0 = _sigmoid(proj(h, OFF_G0, OFF_G1))
        g1_ref[sub * rows:(sub + 1) * rows] = _sigmoid(proj(h, OFF_G1, IN_WIDTH)).astype(BF16)
        pos = pos_ref[:, sub * rows:(sub + 1) * rows].astype(F32)
        ang = jnp.concatenate([invf_ref[...] * pos[:, b * LANES:(b + 1) * LANES]
                               for b in range(rows // LANES)], axis=1)
        n_rot = 2 * HEADS_PER_GROUP
        cos = jnp.concatenate([jnp.cos(ang)] * n_rot + [jnp.ones((LANES - 8 * n_rot, rows), F32)], axis=0).T
        sin = jnp.concatenate([jnp.sin(ang)] * n_rot + [jnp.zeros((LANES - 8 * n_rot, rows), F32)], axis=0).T
        lane = lax.broadcasted_iota(jnp.int32, (rows, LANES), 1)
        sin_a = jnp.where(lane < 32, -sin, 0.0)
        sin_b = jnp.where(lane < 32, 0.0, sin)
        for off, refs, slot0 in ((OFF_Q, q_refs, 0), (OFF_K, k_refs, 2)):
            for g, ref in enumerate(refs):
                lo = off - OFF_Q + g * GROUP_WIDTH
                t = jnp.dot(h[0], wqk_ref[:, lo:lo + GROUP_WIDTH], preferred_element_type=F32) * h[1]
                t0 = t[:, :LANES]
                t0 = t0 * cos + pltpu.roll(t0, LANES - 32, 1) * sin_a + pltpu.roll(t0, 32, 1) * sin_b
                store_by_residue(sub, (t0, t[:, LANES:]), ref, g, slot0 + g - 1)
        for g, ref in enumerate(v_refs):
            t = proj(h, OFF_V + g * GROUP_WIDTH, OFF_V + (g + 1) * GROUP_WIDTH)
            store_by_residue(sub, (t[:, :LANES], t[:, LANES:]), ref, g, 4 + g - 1)
        return g0

    def spatial_gating(sub, u, vn, g0):
        bias = bias_ref[...]
        ya_rows = []
        for c in range(rows // GMLP_CHUNK):
            vc = vn[c * GMLP_CHUNK:(c + 1) * GMLP_CHUNK]
            p0 = jnp.dot(wm[0], vc[:, 0:256], preferred_element_type=F32)
            p1 = jnp.dot(wm[1], vc[:, 128:384], preferred_element_type=F32)
            p2 = jnp.dot(wm[2], vc[:, 384:640], preferred_element_type=F32)
            p3 = jnp.dot(wm[3], vc[:, 512:768], preferred_element_type=F32)
            mixed = jnp.concatenate(
                [p0[:, :LANES], jnp.where(low_half, p0[:, LANES:], p1[:, :LANES]), p1[:, LANES:],
                 p2[:, :LANES], jnp.where(low_half, p2[:, LANES:], p3[:, :LANES]), p3[:, LANES:]],
                axis=1) + bias
            ya_rows.append(u[c * GMLP_CHUNK:(c + 1) * GMLP_CHUNK] * mixed)
        ya = jnp.concatenate(ya_rows, axis=0).astype(BF16)
        ta_ref[sub * rows:(sub + 1) * rows] = (
            g0 * jnp.dot(ya, pa_ref[...], preferred_element_type=F32)).astype(BF16)

    chains = [gating_inputs(sub) for sub in subs]
    gates = [gates_and_qkv(sub, chains[sub][0]) for sub in subs]
    for sub in subs:
        spatial_gating(sub, chains[sub][1], chains[sub][2], gates[sub])


def _resident(shape):
    nd = len(shape)
    return pl.BlockSpec(shape, lambda *_: (0,) * nd, pipeline_mode=pl.Buffered(1))


def _rotary_first_permutation():
    perm = np.zeros((GROUP_WIDTH, GROUP_WIDTH), np.float32)
    for h in range(HEADS_PER_GROUP):
        for dim in range(HEAD_DIM):
            if dim < ROT_HALF:
                dst = ROT_HALF * h + dim
            elif dim < ROT_DIM:
                dst = HEADS_PER_GROUP * ROT_HALF + ROT_HALF * h + (dim - ROT_HALF)
            else:
                dst = HEADS_PER_GROUP * ROT_DIM + PASS_DIM * h + (dim - ROT_DIM)
            perm[h * HEAD_DIM + dim, dst] = 1.0
    return jnp.asarray(perm, BF16)


def _projection(x2, pos2, mix_g, w_in, gmlp_g, w_s, bias_tbl, w_a, inv_freq):
    n = x2.shape[0]
    tm = PROJ_TILE
    row_spec = lambda w: pl.BlockSpec((tm, w), lambda i: (i, 0))
    grp_specs = [pl.BlockSpec((tm // d, d * GROUP_WIDTH), lambda i: (i, 0)) for d in DILATIONS] * 3
    grp_shapes = [jax.ShapeDtypeStruct((n // d, d * GROUP_WIDTH), BF16) for d in DILATIONS] * 3
    wide = jax.ShapeDtypeStruct((n, D_MODEL), BF16)
    return pl.pallas_call(
        _proj_kernel,
        grid=(n // tm,),
        in_specs=[row_spec(D_MODEL), pl.BlockSpec((None, 1, tm), lambda i: (i, 0, 0)),
                  _resident((1, D_MODEL)),
                  _resident((D_MODEL, IN_WIDTH)), _resident((1, GMLP_WIDTH)),
                  _resident((GMLP_GROUPS, GMLP_CHUNK, GMLP_CHUNK)),
                  _resident((GMLP_CHUNK, GMLP_WIDTH)), _resident((GMLP_WIDTH, D_MODEL)),
                  _resident((ROT_HALF, LANES)), _resident((GROUP_WIDTH, GROUP_WIDTH))],
        out_specs=grp_specs + [row_spec(D_MODEL)] * 2,
        out_shape=grp_shapes + [wide] * 2,
        scratch_shapes=[pltpu.VMEM((tm // PROJ_SUBTILE, 6, 2, PROJ_SUBTILE, LANES), F32),
                        pltpu.VMEM((D_MODEL, 2 * len(DILATIONS) * GROUP_WIDTH), BF16),
                        pltpu.VMEM((tm // PROJ_SUBTILE, 3, 2, PROJ_SUBTILE, LANES), F32)],
        compiler_params=pltpu.CompilerParams(
            dimension_semantics=("arbitrary",), vmem_limit_bytes=VMEM_LIMIT_BYTES),
        name="projection",
    )(x2, pos2, mix_g, w_in, gmlp_g, w_s, bias_tbl, w_a, inv_freq, _rotary_first_permutation())


def _head_masks():
    lane = lax.broadcasted_iota(jnp.int32, (1, GROUP_WIDTH), 1)
    qk_head = jnp.where(lane < 2 * 32, (lane % 32) // ROT_HALF, (lane - 2 * 32) // PASS_DIM)
    v_head = lane // HEAD_DIM
    return qk_head, v_head


def _attn_q_masks():
    qk_head, _ = _head_masks()
    heads = jnp.arange(HEADS_PER_GROUP, dtype=jnp.int32)[:, None, None]
    mask = jnp.where(qk_head[None] == heads, 1.0, 0.0)
    return jnp.broadcast_to(mask, (HEADS_PER_GROUP, ATTN_BLOCK, GROUP_WIDTH)).astype(BF16)


def _attn_kernel(q_ref, k_ref, v_ref, qm_ref, *rest, n_res, n_blocks, group, n_casts):
    o_ref, lse_ref = rest[n_casts:n_casts + 2]
    for src, dst in zip(rest[:n_casts], rest[n_casts + 2:]):
        dst[...] = src[...].astype(BF16)
    nh = HEADS_PER_GROUP
    blk = ATTN_BLOCK
    _, v_head = _head_masks()
    qi = lax.broadcasted_iota(jnp.int32, (blk, 2 * blk), 0)
    ki = lax.broadcasted_iota(jnp.int32, (blk, 2 * blk), 1)
    band2 = jnp.where((ki >= qi) & (ki <= qi + BAND), 0.0, -jnp.inf).astype(F32)
    band1 = band2[:, blk:]
    band2 = jnp.concatenate([band2] * nh, axis=0)
    band1 = jnp.concatenate([band1] * nh, axis=0)
    low_half = lax.broadcasted_iota(jnp.int32, (blk, LANES), 1) < HEAD_DIM

    def per_head(x):
        halves = [jnp.where(low_half, x[2 * c * blk:(2 * c + 1) * blk], x[(2 * c + 1) * blk:(2 * c + 2) * blk])
                  for c in range(nh // 2)]
        return jnp.concatenate(halves, axis=1)

    def scores(cols, q_rows, kv_rows, band):
        qb = q_ref[q_rows, cols]
        qs = jnp.concatenate([qb * qm_ref[h] for h in range(nh)], axis=0)
        return lax.dot_general(qs, k_ref[kv_rows, cols], (((1,), (1,)), ((), ())),
                               preferred_element_type=F32) + band

    def softmax(s):
        m = jnp.max(s, axis=-1, keepdims=True)
        p = jnp.exp2(s - m)
        return p.astype(BF16), m, jnp.sum(p, axis=-1, keepdims=True)

    def finish(cols, q_rows, kv_rows, p, m, den):
        pv = jnp.dot(p, v_ref[kv_rows, cols], preferred_element_type=F32)
        out = pv[(nh - 1) * blk:]
        for h in range(nh - 2, -1, -1):
            out = jnp.where(v_head == h, pv[h * blk:(h + 1) * blk], out)
        den = per_head(jnp.broadcast_to(den, (nh * blk, LANES)))
        o_ref[q_rows, cols] = (out / den).astype(o_ref.dtype)
        m_wide = per_head(jnp.broadcast_to(m, (nh * blk, LANES)))
        lse_ref[q_rows, cols] = (m_wide + jnp.log2(den)) * np.float32(np.log(2.0))

    def block_group(cols, blocks):
        s = [scores(cols, *b) for b in blocks]
        pmd = [softmax(x) for x in s]
        for b, (p, m, den) in zip(blocks, pmd):
            finish(cols, b[0], b[1], p, m, den)

    def later_block(q0):
        return slice(q0, q0 + blk), slice(q0 - blk, q0 + blk), band2

    for r in range(n_res):
        cols = slice(r * GROUP_WIDTH, (r + 1) * GROUP_WIDTH)
        first = [(slice(0, blk), slice(0, blk), band1)]
        block_group(cols, first + [later_block(j * blk) for j in range(1, group)])
        for i in range(1, n_blocks // group):
            block_group(cols, [later_block(j * blk) for j in range(i * group, (i + 1) * group)])


def _attention(q, k, v, batch, seq, dil, res_per_step, casts):
    sub = seq // dil
    width = res_per_step * GROUP_WIDTH
    view = lambda t: t.reshape(batch, sub, dil * GROUP_WIDTH)
    spec = pl.BlockSpec((None, sub, width), lambda b, r: (b, 0, r))
    n_blocks = sub // ATTN_BLOCK
    qm = _attn_q_masks()
    cast_specs = [pl.BlockSpec((w.shape[0] // batch, w.shape[1]), lambda b, r: (b, 0)) for w in casts]
    kern = functools.partial(_attn_kernel, n_res=res_per_step, n_blocks=n_blocks,
                             group=min(ATTN_GROUP, n_blocks), n_casts=len(casts))
    o, lse, *cast = pl.pallas_call(
        kern,
        grid=(batch, dil // res_per_step),
        in_specs=[spec, spec, spec, _resident(qm.shape)] + cast_specs,
        out_specs=[spec, spec] + cast_specs,
        out_shape=[jax.ShapeDtypeStruct((batch, sub, dil * GROUP_WIDTH), BF16),
                   jax.ShapeDtypeStruct((batch, sub, dil * GROUP_WIDTH), F32)]
                  + [jax.ShapeDtypeStruct(w.shape, BF16) for w in casts],
        compiler_params=pltpu.CompilerParams(
            dimension_semantics=("arbitrary", "arbitrary"), vmem_limit_bytes=VMEM_LIMIT_BYTES),
        name=f"attention_d{dil}",
    )(view(q), view(k), view(v), qm, *casts)
    return o.reshape(batch * sub, dil * GROUP_WIDTH), lse.reshape(batch * sub, dil * GROUP_WIDTH), cast


def _ffn_kernel(x_ref, ta_ref, g1_ref, o0_ref, o1_ref, o2_ref, l0_ref, l1_ref, l2_ref,
                pb_ref, wo_ref, fg_ref, wup_ref, cw_ref, cb_ref, wd_ref, og_ref,
                out_ref, carry_ref, act_ref, stage_ref, mid_ref):
    tm = x_ref.shape[0]
    rows = FFN_SUBTILE
    subs = range(tm // rows)

    @pl.when(pl.program_id(1) == 0)
    def _():
        carry_ref[...] = jnp.zeros_like(carry_ref)

    def natural(sub, ref, g, slot):
        dil = DILATIONS[g]
        view_rows = slice(sub * rows // dil, (sub + 1) * rows // dil)
        if dil == 1:
            return ref[view_rows, :].astype(F32)
        for r in range(dil):
            for s in range(2):
                piece = ref[view_rows, r * GROUP_WIDTH + s * LANES:r * GROUP_WIDTH + (s + 1) * LANES].astype(F32)
                if dil == 4:
                    stage_ref[sub, slot, s, pl.ds(r, rows // 4, stride=4), :] = piece
                else:
                    r1, r2 = r % 4, r // 4
                    mid_ref[sub, slot // 2, s, pl.ds(r1 * (rows // 4) + r2, rows // 16, stride=4), :] = piece
        if dil == 16:
            for r1 in range(4):
                for s in range(2):
                    stage_ref[sub, slot, s, pl.ds(r1, rows // 4, stride=4), :] = (
                        mid_ref[sub, slot // 2, s, r1 * (rows // 4):(r1 + 1) * (rows // 4), :])
        return jnp.concatenate([stage_ref[sub, slot, 0], stage_ref[sub, slot, 1]], axis=1)

    def merged_branches(sub):
        tok = slice(sub * rows, (sub + 1) * rows)
        l0, l1, l2 = natural(sub, l0_ref, 0, 0), natural(sub, l1_ref, 1, 0), natural(sub, l2_ref, 2, 1)
        lmax = jnp.maximum(jnp.maximum(l0, l1), l2)
        e0, e1, e2 = jnp.exp(l0 - lmax), jnp.exp(l1 - lmax), jnp.exp(l2 - lmax)
        yb = (e0 * natural(sub, o0_ref, 0, 0) + e1 * natural(sub, o1_ref, 1, 2)
              + e2 * natural(sub, o2_ref, 2, 3)) / (e0 + e1 + e2)
        merged = ta_ref[tok, :].astype(F32) + g1_ref[tok, :].astype(F32) * jnp.dot(
            yb.astype(BF16), pb_ref[...], preferred_element_type=F32)
        return merged.astype(BF16)

    def out_proj(sub, merged):
        tok = slice(sub * rows, (sub + 1) * rows)
        x1 = x_ref[tok, :] + jnp.dot(merged, wo_ref[...], preferred_element_type=F32)
        return x1, _rms_deferred(x1, fg_ref[...])

    row8 = lax.broadcasted_iota(jnp.int32, (8, FF_CHUNK), 0)

    def up_chunk(sub, c, h2, prev):
        cols = slice(c * FF_CHUNK, (c + 1) * FF_CHUNK)
        xg, scale = h2
        a = jnp.dot(xg, wup_ref[:, cols], preferred_element_type=F32) * scale
        val = jnp.dot(xg, wup_ref[:, D_FF + c * FF_CHUNK:D_FF + (c + 1) * FF_CHUNK],
                      preferred_element_type=F32) * scale
        s1 = pltpu.roll(a, 1, 0)
        s2 = pltpu.roll(a, 2, 0)
        s1 = jnp.concatenate([jnp.where(row8 < 1, pltpu.roll(prev, 1, 0), s1[:8]), s1[8:]], axis=0)
        s2 = jnp.concatenate([jnp.where(row8 < 2, pltpu.roll(prev, 2, 0), s2[:8]), s2[8:]], axis=0)
        w = cw_ref[:, cols]
        y = s2 * w[0:1] + s1 * w[1:2] + a * w[2:3] + cb_ref[:, cols]
        act_ref[sub * rows:(sub + 1) * rows, cols] = (_gelu(y) * val).astype(BF16)
        return a[rows - 8:]

    merged = [merged_branches(sub) for sub in subs]
    xh = [out_proj(sub, merged[sub]) for sub in subs]
    for c in range(N_FF_CHUNKS):
        cols = slice(c * FF_CHUNK, (c + 1) * FF_CHUNK)
        prev = carry_ref[:, cols]
        for sub in subs:
            prev = up_chunk(sub, c, xh[sub][1], prev)
        carry_ref[:, cols] = prev
    for sub in subs:
        tok = slice(sub * rows, (sub + 1) * rows)
        x2 = xh[sub][0] + jnp.dot(act_ref[tok, :], wd_ref[...], preferred_element_type=F32)
        out_ref[tok, :] = _rms(x2, og_ref[...])


def _ffn(x, ta, g1, os_, ls_, w_b, w_out, ffn_g, w_up, conv_w, conv_b, w_down, final_g, batch, seq):
    tm = FFN_TILE
    tiles = seq // tm
    row_spec = lambda w: pl.BlockSpec((tm, w), lambda b, i: (b * tiles + i, 0))
    grp_specs = [pl.BlockSpec((tm // d, d * GROUP_WIDTH), lambda b, i: (b * tiles + i, 0))
                 for d in DILATIONS]
    return pl.pallas_call(
        _ffn_kernel,
        grid=(batch, tiles),
        in_specs=[row_spec(D_MODEL), row_spec(D_MODEL), row_spec(D_MODEL)]
                 + grp_specs * 2
                 + [_resident((GROUP_WIDTH, D_MODEL)), _resident((D_MODEL, D_MODEL)),
                    _resident((1, D_MODEL)), _resident((D_MODEL, 2 * D_FF)),
                    _resident((3, D_FF)), _resident((1, D_FF)), _resident((D_FF, D_MODEL)),
                    _resident((1, D_MODEL))],
        out_specs=row_spec(D_MODEL),
        out_shape=jax.ShapeDtypeStruct((batch * seq, D_MODEL), F32),
        scratch_shapes=[pltpu.VMEM((8, D_FF), F32), pltpu.VMEM((tm, D_FF), BF16),
                        pltpu.VMEM((tm // FFN_SUBTILE, 4, 2, FFN_SUBTILE, LANES), F32),
                        pltpu.VMEM((tm // FFN_SUBTILE, 2, 2, FFN_SUBTILE, LANES), F32)],
        compiler_params=pltpu.CompilerParams(
            dimension_semantics=("arbitrary", "arbitrary"), vmem_limit_bytes=VMEM_LIMIT_BYTES),
        name="mix_ffn",
    )(x, ta, g1, *os_, *ls_, w_b, w_out, ffn_g, w_up, conv_w, conv_b, w_down, final_g)


def kernel(x, positions, mix_norm_g, w_in, gmlp_norm_g, w_spatial, b_spatial, w_branch_a,
           w_branch_b, w_out, ffn_norm_g, w_up, conv_w, conv_b, w_down, final_norm_g):
    batch, seq, d = x.shape
    assert w_in.shape[0] == 1, "single-layer block: the final norm is fused into the FFN stage"
    layer = 0
    n = batch * seq
    inv_freq = ROPE_THETA ** (-jnp.arange(0, ROT_DIM, 2, dtype=F32) / ROT_DIM)
    inv_freq = jnp.broadcast_to(inv_freq[:, None], (ROT_HALF, LANES))
    pos2 = positions.reshape(n // PROJ_TILE, 1, PROJ_TILE)
    xf = x.reshape(n, d)
    col_scale = np.ones((IN_WIDTH,), np.float32)
    col_scale[OFF_Q:OFF_K] = HEAD_DIM ** -0.5
    col_scale[OFF_K:OFF_V] = np.log2(np.e)
    wi = (w_in[layer] * col_scale).astype(BF16)
    bias_tbl = jnp.repeat(b_spatial[layer].T, GMLP_WIDTH // GMLP_GROUPS, axis=1)
    res = _projection(xf, pos2, mix_norm_g[layer].reshape(1, d), wi,
                      gmlp_norm_g[layer].reshape(1, GMLP_WIDTH), w_spatial[layer], bias_tbl,
                      w_branch_a[layer].astype(BF16), inv_freq)
    qs, ks, vs, ta, g1 = res[0:3], res[3:6], res[6:9], res[9], res[10]
    side_casts = ([w_up[layer]], [w_down[layer]], [w_out[layer], w_branch_b[layer]])
    outs, lses, cast = [], [], []
    for g, (dil, rps) in enumerate(zip(DILATIONS, RESIDUES_PER_STEP)):
        o, lse, c = _attention(qs[g], ks[g], vs[g], batch, seq, dil, rps, side_casts[g])
        outs.append(o)
        lses.append(lse)
        cast += c
    w_up_b, w_down_b, w_out_b, w_b_b = cast
    out = _ffn(xf, ta, g1, outs, lses, w_b_b, w_out_b,
               ffn_norm_g[layer].reshape(1, d), w_up_b, conv_w[layer],
               conv_b[layer].reshape(1, D_FF), w_down_b,
               final_norm_g.reshape(1, d), batch, seq)
    return out.reshape(batch, seq, d)
```

```python
import functools

import jax
import jax.numpy as jnp
import numpy as np
from jax import lax
from jax.experimental import pallas as pl
from jax.experimental.pallas import tpu as pltpu

F32 = jnp.float32
BF16 = jnp.bfloat16

D_MODEL = 1024
EPS = 1e-6
GMLP_WIDTH = 768
GMLP_GROUPS = 4
GMLP_CHUNK = 128
HEAD_DIM = 64
HEADS_PER_GROUP = 4
GROUP_WIDTH = HEADS_PER_GROUP * HEAD_DIM
DILATIONS = (1, 4, 16)
RESIDUES_PER_STEP = (1, 4, 16)
BAND = 128
ATTN_BLOCK = 128
ATTN_GROUP = 4
ROPE_THETA = 500000.0
ROT_DIM = HEAD_DIM // 4
ROT_HALF = ROT_DIM // 2
PASS_DIM = HEAD_DIM - ROT_DIM
D_FF = 2816
FF_CHUNK = 256
N_FF_CHUNKS = D_FF // FF_CHUNK

OFF_Q = 2 * GMLP_WIDTH
OFF_K = OFF_Q + 3 * GROUP_WIDTH
OFF_V = OFF_K + 3 * GROUP_WIDTH
OFF_G0 = OFF_V + 3 * GROUP_WIDTH
OFF_G1 = OFF_G0 + D_MODEL
IN_WIDTH = OFF_G1 + D_MODEL

LANES = 128
PROJ_TILE = 512
PROJ_SUBTILE = 256
FFN_TILE = 512
FFN_SUBTILE = 256
VMEM_LIMIT_BYTES = 56 * 1024 * 1024


def _erf(x):
    return lax.erf(x)


def _gelu(x):
    return 0.5 * x * (1.0 + _erf(x * np.float32(np.sqrt(0.5))))


def _sigmoid(x):
    return 1.0 / (1.0 + jnp.exp(-x))


def _rms(x, g):
    ms = jnp.mean(x * x, axis=-1, keepdims=True)
    return x * lax.rsqrt(ms + EPS) * g


def _rms_deferred(x, g):
    ms = jnp.mean(x * x, axis=-1, keepdims=True)
    return (x * g).astype(BF16), lax.rsqrt(ms + EPS)


def _proj_kernel(x_ref, pos_ref, g_ref, w_ref, gg_ref, ws_ref, bias_ref, pa_ref, invf_ref, perm_ref,
                 q0_ref, q1_ref, q2_ref, k0_ref, k1_ref, k2_ref, v0_ref, v1_ref, v2_ref,
                 ta_ref, g1_ref, stage_ref, wqk_ref, mid_ref):
    tm = x_ref.shape[0]
    rows = PROJ_SUBTILE
    subs = range(tm // rows)

    @pl.when(pl.program_id(0) == 0)
    def _():
        for j in range(2 * len(DILATIONS)):
            cols = slice(j * GROUP_WIDTH, (j + 1) * GROUP_WIDTH)
            wqk_ref[:, cols] = jnp.dot(w_ref[:, OFF_Q + j * GROUP_WIDTH:OFF_Q + (j + 1) * GROUP_WIDTH],
                                       perm_ref[...], preferred_element_type=F32).astype(BF16)

    q_refs, k_refs, v_refs = (q0_ref, q1_ref, q2_ref), (k0_ref, k1_ref, k2_ref), (v0_ref, v1_ref, v2_ref)
    row = lax.broadcasted_iota(jnp.int32, (GMLP_CHUNK, GMLP_CHUNK), 0)
    col = lax.broadcasted_iota(jnp.int32, (GMLP_CHUNK, GMLP_CHUNK), 1)
    wm = [jnp.where(row >= col, ws_ref[g], 0.0).astype(BF16) for g in range(GMLP_GROUPS)]
    low_half = lax.broadcasted_iota(jnp.int32, (GMLP_CHUNK, LANES), 1) < (LANES // 2)

    def proj(h, lo, hi):
        xg, scale = h
        return jnp.dot(xg, w_ref[:, lo:hi], preferred_element_type=F32) * scale

    def gating_inputs(sub):
        h = _rms_deferred(x_ref[sub * rows:(sub + 1) * rows], g_ref[...])
        vn = _rms(_gelu(proj(h, GMLP_WIDTH, 2 * GMLP_WIDTH)), gg_ref[...]).astype(BF16)
        u = _gelu(proj(h, 0, GMLP_WIDTH))
        return h, u, vn

    def store_by_residue(sub, halves, ref, g, slot):
        dil = DILATIONS[g]
        out_rows = slice(sub * rows // dil, (sub + 1) * rows // dil)
        if dil == 1:
            for s, half in enumerate(halves):
                ref[out_rows, s * LANES:(s + 1) * LANES] = half.astype(BF16)
            return
        for s, half in enumerate(halves):
            stage_ref[sub, slot, s] = half
        if dil == 16:
            for r1 in range(4):
                for s in range(2):
                    mid_ref[sub, slot // 2, s, r1 * (rows // 4):(r1 + 1) * (rows // 4), :] = stage_ref[
                        sub, slot, s, pl.ds(r1, rows // 4, stride=4), :]
        for r in range(dil):
            for s in range(2):
                lo = r * GROUP_WIDTH + s * LANES
                if dil == 4:
                    piece = stage_ref[sub, slot, s, pl.ds(r, rows // 4, stride=4), :]
                else:
                    piece = mid_ref[sub, slot // 2, s,
                                    pl.ds((r % 4) * (rows // 4) + r // 4, rows // 16, stride=4), :]
                ref[out_rows, lo:lo + LANES] = piece.astype(BF16)

    def gates_and_qkv(sub, h):
        g0 = _sigmoid(proj(h, OFF_G0, OFF_G1))
        g1_ref[sub * rows:(sub + 1) * rows] = _sigmoid(proj(h, OFF_G1, IN_WIDTH)).astype(BF16)
        pos = pos_ref[:, sub * rows:(sub + 1) * rows].astype(F32)
        ang = jnp.concatenate([invf_ref[...] * pos[:, b * LANES:(b + 1) * LANES]
                               for b in range(rows // LANES)], axis=1)
        n_rot = 2 * HEADS_PER_GROUP
        cos = jnp.concatenate([jnp.cos(ang)] * n_rot + [jnp.ones((LANES - 8 * n_rot, rows), F32)], axis=0).T
        sin = jnp.concatenate([jnp.sin(ang)] * n_rot + [jnp.zeros((LANES - 8 * n_rot, rows), F32)], axis=0).T
        lane = lax.broadcasted_iota(jnp.int32, (rows, LANES), 1)
        sin_a = jnp.where(lane < 32, -sin, 0.0)
        sin_b = jnp.where(lane < 32, 0.0, sin)
        for off, refs, slot0 in ((OFF_Q, q_refs, 0), (OFF_K, k_refs, 2)):
            for g, ref in enumerate(refs):
                lo = off - OFF_Q + g * GROUP_WIDTH
                t = jnp.dot(h[0], wqk_ref[:, lo:lo + GROUP_WIDTH], preferred_element_type=F32) * h[1]
                t0 = t[:, :LANES]
                t0 = t0 * cos + pltpu.roll(t0, LANES - 32, 1) * sin_a + pltpu.roll(t0, 32, 1) * sin_b
                store_by_residue(sub, (t0, t[:, LANES:]), ref, g, slot0 + g - 1)
        for g, ref in enumerate(v_refs):
            t = proj(h, OFF_V + g * GROUP_WIDTH, OFF_V + (g + 1) * GROUP_WIDTH)
            store_by_residue(sub, (t[:, :LANES], t[:, LANES:]), ref, g, 4 + g - 1)
        return g0

    def spatial_gating(sub, u, vn, g0):
        bias = bias_ref[...]
        ya_rows = []
        for c in range(rows // GMLP_CHUNK):
            vc = vn[c * GMLP_CHUNK:(c + 1) * GMLP_CHUNK]
            p0 = jnp.dot(wm[0], vc[:, 0:256], preferred_element_type=F32)
            p1 = jnp.dot(wm[1], vc[:, 128:384], preferred_element_type=F32)
            p2 = jnp.dot(wm[2], vc[:, 384:640], preferred_element_type=F32)
            p3 = jnp.dot(wm[3], vc[:, 512:768], preferred_element_type=F32)
            mixed = jnp.concatenate(
                [p0[:, :LANES], jnp.where(low_half, p0[:, LANES:], p1[:, :LANES]), p1[:, LANES:],
                 p2[:, :LANES], jnp.where(low_half, p2[:, LANES:], p3[:, :LANES]), p3[:, LANES:]],
                axis=1) + bias
            ya_rows.append(u[c * GMLP_CHUNK:(c + 1) * GMLP_CHUNK] * mixed)
        ya = jnp.concatenate(ya_rows, axis=0).astype(BF16)
        ta_ref[sub * rows:(sub + 1) * rows] = (
            g0 * jnp.dot(ya, pa_ref[...], preferred_element_type=F32)).astype(BF16)

    chains = [gating_inputs(sub) for sub in subs]
    gates = [gates_and_qkv(sub, chains[sub][0]) for sub in subs]
    for sub in subs:
        spatial_gating(sub, chains[sub][1], chains[sub][2], gates[sub])


def _resident(shape):
    nd = len(shape)
    return pl.BlockSpec(shape, lambda *_: (0,) * nd, pipeline_mode=pl.Buffered(1))


def _rotary_first_permutation():
    perm = np.zeros((GROUP_WIDTH, GROUP_WIDTH), np.float32)
    for h in range(HEADS_PER_GROUP):
        for dim in range(HEAD_DIM):
            if dim < ROT_HALF:
                dst = ROT_HALF * h + dim
            elif dim < ROT_DIM:
                dst = HEADS_PER_GROUP * ROT_HALF + ROT_HALF * h + (dim - ROT_HALF)
            else:
                dst = HEADS_PER_GROUP * ROT_DIM + PASS_DIM * h + (dim - ROT_DIM)
            perm[h * HEAD_DIM + dim, dst] = 1.0
    return jnp.asarray(perm, BF16)


def _projection(x2, pos2, mix_g, w_in, gmlp_g, w_s, bias_tbl, w_a, inv_freq):
    n = x2.shape[0]
    tm = PROJ_TILE
    row_spec = lambda w: pl.BlockSpec((tm, w), lambda i: (i, 0))
    grp_specs = [pl.BlockSpec((tm // d, d * GROUP_WIDTH), lambda i: (i, 0)) for d in DILATIONS] * 3
    grp_shapes = [jax.ShapeDtypeStruct((n // d, d * GROUP_WIDTH), BF16) for d in DILATIONS] * 3
    wide = jax.ShapeDtypeStruct((n, D_MODEL), BF16)
    return pl.pallas_call(
        _proj_kernel,
        grid=(n // tm,),
        in_specs=[row_spec(D_MODEL), pl.BlockSpec((None, 1, tm), lambda i: (i, 0, 0)),
                  _resident((1, D_MODEL)),
                  _resident((D_MODEL, IN_WIDTH)), _resident((1, GMLP_WIDTH)),
                  _resident((GMLP_GROUPS, GMLP_CHUNK, GMLP_CHUNK)),
                  _resident((GMLP_CHUNK, GMLP_WIDTH)), _resident((GMLP_WIDTH, D_MODEL)),
                  _resident((ROT_HALF, LANES)), _resident((GROUP_WIDTH, GROUP_WIDTH))],
        out_specs=grp_specs + [row_spec(D_MODEL)] * 2,
        out_shape=grp_shapes + [wide] * 2,
        scratch_shapes=[pltpu.VMEM((tm // PROJ_SUBTILE, 6, 2, PROJ_SUBTILE, LANES), F32),
                        pltpu.VMEM((D_MODEL, 2 * len(DILATIONS) * GROUP_WIDTH), BF16),
                        pltpu.VMEM((tm // PROJ_SUBTILE, 3, 2, PROJ_SUBTILE, LANES), F32)],
        compiler_params=pltpu.CompilerParams(
            dimension_semantics=("arbitrary",), vmem_limit_bytes=VMEM_LIMIT_BYTES),
        name="projection",
    )(x2, pos2, mix_g, w_in, gmlp_g, w_s, bias_tbl, w_a, inv_freq, _rotary_first_permutation())


def _head_masks():
    lane = lax.broadcasted_iota(jnp.int32, (1, GROUP_WIDTH), 1)
    qk_head = jnp.where(lane < 2 * 32, (lane % 32) // ROT_HALF, (lane - 2 * 32) // PASS_DIM)
    v_head = lane // HEAD_DIM
    return qk_head, v_head


def _attn_kernel(q_ref, k_ref, v_ref, *rest, n_res, n_blocks, group, n_casts):
    o_ref, lse_ref = rest[n_casts:n_casts + 2]
    for src, dst in zip(rest[:n_casts], rest[n_casts + 2:]):
        dst[...] = src[...].astype(BF16)
    nh = HEADS_PER_GROUP
    blk = ATTN_BLOCK
    qk_head, v_head = _head_masks()
    qi = lax.broadcasted_iota(jnp.int32, (blk, 2 * blk), 0)
    ki = lax.broadcasted_iota(jnp.int32, (blk, 2 * blk), 1)
    band2 = jnp.where((ki >= qi) & (ki <= qi + BAND), 0.0, -jnp.inf).astype(F32)
    band1 = band2[:, blk:]
    band2 = jnp.concatenate([band2] * nh, axis=0)
    band1 = jnp.concatenate([band1] * nh, axis=0)
    low_half = lax.broadcasted_iota(jnp.int32, (blk, LANES), 1) < HEAD_DIM

    def per_head(x):
        halves = [jnp.where(low_half, x[2 * c * blk:(2 * c + 1) * blk], x[(2 * c + 1) * blk:(2 * c + 2) * blk])
                  for c in range(nh // 2)]
        return jnp.concatenate(halves, axis=1)

    def scores(cols, q_rows, kv_rows, band):
        qb = q_ref[q_rows, cols]
        qs = jnp.concatenate([jnp.where(qk_head == h, qb, jnp.zeros_like(qb)) for h in range(nh)], axis=0)
        return lax.dot_general(qs, k_ref[kv_rows, cols], (((1,), (1,)), ((), ())),
                               preferred_element_type=F32) + band

    def softmax(s):
        m = jnp.max(s, axis=-1, keepdims=True)
        p = jnp.exp2(s - m)
        return p.astype(BF16), m, jnp.sum(p, axis=-1, keepdims=True)

    def finish(cols, q_rows, kv_rows, p, m, den):
        pv = jnp.dot(p, v_ref[kv_rows, cols], preferred_element_type=F32)
        out = pv[(nh - 1) * blk:]
        for h in range(nh - 2, -1, -1):
            out = jnp.where(v_head == h, pv[h * blk:(h + 1) * blk], out)
        den = per_head(jnp.broadcast_to(den, (nh * blk, LANES)))
        o_ref[q_rows, cols] = (out / den).astype(o_ref.dtype)
        m_wide = per_head(jnp.broadcast_to(m, (nh * blk, LANES)))
        lse_ref[q_rows, cols] = (m_wide + jnp.log2(den)) * np.float32(np.log(2.0))

    def block_group(cols, blocks):
        s = [scores(cols, *b) for b in blocks]
        pmd = [softmax(x) for x in s]
        for b, (p, m, den) in zip(blocks, pmd):
            finish(cols, b[0], b[1], p, m, den)

    def later_block(q0):
        return slice(q0, q0 + blk), slice(q0 - blk, q0 + blk), band2

    for r in range(n_res):
        cols = slice(r * GROUP_WIDTH, (r + 1) * GROUP_WIDTH)
        first = [(slice(0, blk), slice(0, blk), band1)]
        block_group(cols, first + [later_block(j * blk) for j in range(1, group)])
        for i in range(1, n_blocks // group):
            block_group(cols, [later_block(j * blk) for j in range(i * group, (i + 1) * group)])


def _attention(q, k, v, batch, seq, dil, res_per_step, casts):
    sub = seq // dil
    width = res_per_step * GROUP_WIDTH
    view = lambda t: t.reshape(batch, sub, dil * GROUP_WIDTH)
    spec = pl.BlockSpec((None, sub, width), lambda b, r: (b, 0, r))
    n_blocks = sub // ATTN_BLOCK
    cast_specs = [pl.BlockSpec((w.shape[0] // batch, w.shape[1]), lambda b, r: (b, 0)) for w in casts]
    kern = functools.partial(_attn_kernel, n_res=res_per_step, n_blocks=n_blocks,
                             group=min(ATTN_GROUP, n_blocks), n_casts=len(casts))
    o, lse, *cast = pl.pallas_call(
        kern,
        grid=(batch, dil // res_per_step),
        in_specs=[spec, spec, spec] + cast_specs,
        out_specs=[spec, spec] + cast_specs,
        out_shape=[jax.ShapeDtypeStruct((batch, sub, dil * GROUP_WIDTH), BF16),
                   jax.ShapeDtypeStruct((batch, sub, dil * GROUP_WIDTH), F32)]
                  + [jax.ShapeDtypeStruct(w.shape, BF16) for w in casts],
        compiler_params=pltpu.CompilerParams(
            dimension_semantics=("arbitrary", "arbitrary"), vmem_limit_bytes=VMEM_LIMIT_BYTES),
        name=f"attention_d{dil}",
    )(view(q), view(k), view(v), *casts)
    return o.reshape(batch * sub, dil * GROUP_WIDTH), lse.reshape(batch * sub, dil * GROUP_WIDTH), cast


def _ffn_kernel(x_ref, ta_ref, g1_ref, o0_ref, o1_ref, o2_ref, l0_ref, l1_ref, l2_ref,
                pb_ref, wo_ref, fg_ref, wup_ref, cw_ref, cb_ref, wd_ref, og_ref,
                out_ref, carry_ref, act_ref, stage_ref, mid_ref):
    tm = x_ref.shape[0]
    rows = FFN_SUBTILE
    subs = range(tm // rows)

    @pl.when(pl.program_id(1) == 0)
    def _():
        carry_ref[...] = jnp.zeros_like(carry_ref)

    def natural(sub, ref, g, slot):
        dil = DILATIONS[g]
        view_rows = slice(sub * rows // dil, (sub + 1) * rows // dil)
        if dil == 1:
            return ref[view_rows, :].astype(F32)
        for r in range(dil):
            for s in range(2):
                piece = ref[view_rows, r * GROUP_WIDTH + s * LANES:r * GROUP_WIDTH + (s + 1) * LANES].astype(F32)
                if dil == 4:
                    stage_ref[sub, slot, s, pl.ds(r, rows // 4, stride=4), :] = piece
                else:
                    r1, r2 = r % 4, r // 4
                    mid_ref[sub, slot // 2, s, pl.ds(r1 * (rows // 4) + r2, rows // 16, stride=4), :] = piece
        if dil == 16:
            for r1 in range(4):
                for s in range(2):
                    stage_ref[sub, slot, s, pl.ds(r1, rows // 4, stride=4), :] = (
                        mid_ref[sub, slot // 2, s, r1 * (rows // 4):(r1 + 1) * (rows // 4), :])
        return jnp.concatenate([stage_ref[sub, slot, 0], stage_ref[sub, slot, 1]], axis=1)

    def merged_branches(sub):
        tok = slice(sub * rows, (sub + 1) * rows)
        l0, l1, l2 = natural(sub, l0_ref, 0, 0), natural(sub, l1_ref, 1, 0), natural(sub, l2_ref, 2, 1)
        lmax = jnp.maximum(jnp.maximum(l0, l1), l2)
        e0, e1, e2 = jnp.exp(l0 - lmax), jnp.exp(l1 - lmax), jnp.exp(l2 - lmax)
        yb = (e0 * natural(sub, o0_ref, 0, 0) + e1 * natural(sub, o1_ref, 1, 2)
              + e2 * natural(sub, o2_ref, 2, 3)) / (e0 + e1 + e2)
        merged = ta_ref[tok, :].astype(F32) + g1_ref[tok, :].astype(F32) * jnp.dot(
            yb.astype(BF16), pb_ref[...], preferred_element_type=F32)
        return merged.astype(BF16)

    def out_proj(sub, merged):
        tok = slice(sub * rows, (sub + 1) * rows)
        x1 = x_ref[tok, :] + jnp.dot(merged, wo_ref[...], preferred_element_type=F32)
        return x1, _rms_deferred(x1, fg_ref[...])

    row8 = lax.broadcasted_iota(jnp.int32, (8, FF_CHUNK), 0)

    def up_chunk(sub, c, h2, prev):
        cols = slice(c * FF_CHUNK, (c + 1) * FF_CHUNK)
        xg, scale = h2
        a = jnp.dot(xg, wup_ref[:, cols], preferred_element_type=F32) * scale
        val = jnp.dot(xg, wup_ref[:, D_FF + c * FF_CHUNK:D_FF + (c + 1) * FF_CHUNK],
                      preferred_element_type=F32) * scale
        s1 = pltpu.roll(a, 1, 0)
        s2 = pltpu.roll(a, 2, 0)
        s1 = jnp.concatenate([jnp.where(row8 < 1, pltpu.roll(prev, 1, 0), s1[:8]), s1[8:]], axis=0)
        s2 = jnp.concatenate([jnp.where(row8 < 2, pltpu.roll(prev, 2, 0), s2[:8]), s2[8:]], axis=0)
        w = cw_ref[:, cols]
        y = s2 * w[0:1] + s1 * w[1:2] + a * w[2:3] + cb_ref[:, cols]
        act_ref[sub * rows:(sub + 1) * rows, cols] = (_gelu(y) * val).astype(BF16)
        return a[rows - 8:]

    merged = [merged_branches(sub) for sub in subs]
    xh = [out_proj(sub, merged[sub]) for sub in subs]
    for c in range(N_FF_CHUNKS):
        cols = slice(c * FF_CHUNK, (c + 1) * FF_CHUNK)
        prev = carry_ref[:, cols]
        for sub in subs:
            prev = up_chunk(sub, c, xh[sub][1], prev)
        carry_ref[:, cols] = prev
    for sub in subs:
        tok = slice(sub * rows, (sub + 1) * rows)
        x2 = xh[sub][0] + jnp.dot(act_ref[tok, :], wd_ref[...], preferred_element_type=F32)
        out_ref[tok, :] = _rms(x2, og_ref[...])


def _ffn(x, ta, g1, os_, ls_, w_b, w_out, ffn_g, w_up, conv_w, conv_b, w_down, final_g, batch, seq):
    tm = FFN_TILE
    tiles = seq // tm
    row_spec = lambda w: pl.BlockSpec((tm, w), lambda b, i: (b * tiles + i, 0))
    grp_specs = [pl.BlockSpec((tm // d, d * GROUP_WIDTH), lambda b, i: (b * tiles + i, 0))
                 for d in DILATIONS]
    return pl.pallas_call(
        _ffn_kernel,
        grid=(batch, tiles),
        in_specs=[row_spec(D_MODEL), row_spec(D_MODEL), row_spec(D_MODEL)]
                 + grp_specs * 2
                 + [_resident((GROUP_WIDTH, D_MODEL)), _resident((D_MODEL, D_MODEL)),
                    _resident((1, D_MODEL)), _resident((D_MODEL, 2 * D_FF)),
                    _resident((3, D_FF)), _resident((1, D_FF)), _resident((D_FF, D_MODEL)),
                    _resident((1, D_MODEL))],
        out_specs=row_spec(D_MODEL),
        out_shape=jax.ShapeDtypeStruct((batch * seq, D_MODEL), F32),
        scratch_shapes=[pltpu.VMEM((8, D_FF), F32), pltpu.VMEM((tm, D_FF), BF16),
                        pltpu.VMEM((tm // FFN_SUBTILE, 4, 2, FFN_SUBTILE, LANES), F32),
                        pltpu.VMEM((tm // FFN_SUBTILE, 2, 2, FFN_SUBTILE, LANES), F32)],
        compiler_params=pltpu.CompilerParams(
            dimension_semantics=("arbitrary", "arbitrary"), vmem_limit_bytes=VMEM_LIMIT_BYTES),
        name="mix_ffn",
    )(x, ta, g1, *os_, *ls_, w_b, w_out, ffn_g, w_up, conv_w, conv_b, w_down, final_g)


def kernel(x, positions, mix_norm_g, w_in, gmlp_norm_g, w_spatial, b_spatial, w_branch_a,
           w_branch_b, w_out, ffn_norm_g, w_up, conv_w, conv_b, w_down, final_norm_g):
    batch, seq, d = x.shape
    assert w_in.shape[0] == 1, "single-layer block: the final norm is fused into the FFN stage"
    layer = 0
    n = batch * seq
    inv_freq = ROPE_THETA ** (-jnp.arange(0, ROT_DIM, 2, dtype=F32) / ROT_DIM)
    inv_freq = jnp.broadcast_to(inv_freq[:, None], (ROT_HALF, LANES))
    pos2 = positions.reshape(n // PROJ_TILE, 1, PROJ_TILE)
    xf = x.reshape(n, d)
    col_scale = np.ones((IN_WIDTH,), np.float32)
    col_scale[OFF_Q:OFF_K] = HEAD_DIM ** -0.5
    col_scale[OFF_K:OFF_V] = np.log2(np.e)
    wi = (w_in[layer] * col_scale).astype(BF16)
    bias_tbl = jnp.repeat(b_spatial[layer].T, GMLP_WIDTH // GMLP_GROUPS, axis=1)
    res = _projection(xf, pos2, mix_norm_g[layer].reshape(1, d), wi,
                      gmlp_norm_g[layer].reshape(1, GMLP_WIDTH), w_spatial[layer], bias_tbl,
                      w_branch_a[layer].astype(BF16), inv_freq)
    qs, ks, vs, ta, g1 = res[0:3], res[3:6], res[6:9], res[9], res[10]
    side_casts = ([w_up[layer]], [w_down[layer]], [w_out[layer], w_branch_b[layer]])
    outs, lses, cast = [], [], []
    for g, (dil, rps) in enumerate(zip(DILATIONS, RESIDUES_PER_STEP)):
        o, lse, c = _attention(qs[g], ks[g], vs[g], batch, seq, dil, rps, side_casts[g])
        outs.append(o)
        lses.append(lse)
        cast += c
    w_up_b, w_down_b, w_out_b, w_b_b = cast
    out = _ffn(xf, ta, g1, outs, lses, w_b_b, w_out_b,
               ffn_norm_g[layer].reshape(1, d), w_up_b, conv_w[layer],
               conv_b[layer].reshape(1, D_FF), w_down_b,
               final_norm_g.reshape(1, d), batch, seq)
    return out.reshape(batch, seq, d)
```

```python
import functools

import jax
import jax.numpy as jnp
import numpy as np
from jax import lax
from jax.experimental import pallas as pl
from jax.experimental.pallas import tpu as pltpu

F32 = jnp.float32
BF16 = jnp.bfloat16

D_MODEL = 1024
EPS = 1e-6
GMLP_WIDTH = 768
GMLP_GROUPS = 4
GMLP_CHUNK = 128
HEAD_DIM = 64
HEADS_PER_GROUP = 4
GROUP_WIDTH = HEADS_PER_GROUP * HEAD_DIM
QKV_WIDTH = 3 * GROUP_WIDTH
DILATIONS = (1, 4, 16)
RESIDUES_PER_STEP = (1, 4, 16)
BAND = 128
ATTN_BLOCK = 128
ATTN_GROUP = 4
ROPE_THETA = 500000.0
ROT_DIM = HEAD_DIM // 4
ROT_HALF = ROT_DIM // 2
PASS_DIM = HEAD_DIM - ROT_DIM
D_FF = 2816
FF_CHUNK = 256
N_FF_CHUNKS = D_FF // FF_CHUNK

OFF_Q = 2 * GMLP_WIDTH
OFF_K = OFF_Q + 3 * GROUP_WIDTH
OFF_V = OFF_K + 3 * GROUP_WIDTH
OFF_G0 = OFF_V + 3 * GROUP_WIDTH
OFF_G1 = OFF_G0 + D_MODEL
IN_WIDTH = OFF_G1 + D_MODEL

LANES = 128
PROJ_TILE = 1024
PROJ_SUBTILE = 256
STAGE_SETS = 2
FFN_TILE = 512
FFN_SUBTILE = 256
VMEM_LIMIT_BYTES = 56 * 1024 * 1024


def _erf(x):
    return lax.erf(x)


def _gelu(x):
    return 0.5 * x * (1.0 + _erf(x * np.float32(np.sqrt(0.5))))


def _sigmoid(x):
    return 1.0 / (1.0 + jnp.exp(-x))


def _rms(x, g):
    ms = jnp.mean(x * x, axis=-1, keepdims=True)
    return x * lax.rsqrt(ms + EPS) * g


def _rms_deferred(x, g):
    ms = jnp.mean(x * x, axis=-1, keepdims=True)
    return (x * g).astype(BF16), lax.rsqrt(ms + EPS)


def _proj_kernel(x_ref, pos_ref, g_ref, w_ref, gg_ref, ws_ref, bias_ref, pa_ref, invf_ref, perm_ref,
                 qkv0_ref, qkv1_ref, qkv2_ref,
                 ta_ref, g1_ref, stage_ref, wqk_ref, mid_ref):
    tm = x_ref.shape[0]
    rows = PROJ_SUBTILE
    subs = range(tm // rows)

    @pl.when(pl.program_id(0) == 0)
    def _():
        for j in range(2 * len(DILATIONS)):
            cols = slice(j * GROUP_WIDTH, (j + 1) * GROUP_WIDTH)
            wqk_ref[:, cols] = jnp.dot(w_ref[:, OFF_Q + j * GROUP_WIDTH:OFF_Q + (j + 1) * GROUP_WIDTH],
                                       perm_ref[...], preferred_element_type=F32).astype(BF16)

    qkv_refs = (qkv0_ref, qkv1_ref, qkv2_ref)
    row = lax.broadcasted_iota(jnp.int32, (GMLP_CHUNK, GMLP_CHUNK), 0)
    col = lax.broadcasted_iota(jnp.int32, (GMLP_CHUNK, GMLP_CHUNK), 1)
    wm = [jnp.where(row >= col, ws_ref[g], 0.0).astype(BF16) for g in range(GMLP_GROUPS)]
    low_half = lax.broadcasted_iota(jnp.int32, (GMLP_CHUNK, LANES), 1) < (LANES // 2)

    def proj(h, lo, hi):
        xg, scale = h
        return jnp.dot(xg, w_ref[:, lo:hi], preferred_element_type=F32) * scale

    def gating_inputs(sub):
        h = _rms_deferred(x_ref[sub * rows:(sub + 1) * rows], g_ref[...])
        vn = _rms(_gelu(proj(h, GMLP_WIDTH, 2 * GMLP_WIDTH)), gg_ref[...]).astype(BF16)
        u = _gelu(proj(h, 0, GMLP_WIDTH))
        return h, u, vn

    def store_by_residue(sub, halves, which, g, slot):
        dil = DILATIONS[g]
        ref = qkv_refs[g]
        out_rows = slice(sub * rows // dil, (sub + 1) * rows // dil)
        if dil == 1:
            for s, half in enumerate(halves):
                lo = which * GROUP_WIDTH + s * LANES
                ref[out_rows, lo:lo + LANES] = half.astype(BF16)
            return
        buf = sub % STAGE_SETS
        for s, half in enumerate(halves):
            stage_ref[buf, slot, s] = half
        if dil == 16:
            for r1 in range(4):
                for s in range(2):
                    mid_ref[buf, slot // 2, s, r1 * (rows // 4):(r1 + 1) * (rows // 4), :] = stage_ref[
                        buf, slot, s, pl.ds(r1, rows // 4, stride=4), :]
        for r in range(dil):
            for s in range(2):
                lo = r * QKV_WIDTH + which * GROUP_WIDTH + s * LANES
                if dil == 4:
                    piece = stage_ref[buf, slot, s, pl.ds(r, rows // 4, stride=4), :]
                else:
                    piece = mid_ref[buf, slot // 2, s,
                                    pl.ds((r % 4) * (rows // 4) + r // 4, rows // 16, stride=4), :]
                ref[out_rows, lo:lo + LANES] = piece.astype(BF16)

    def gates_and_qkv(sub, h):
        g0 = _sigmoid(proj(h, OFF_G0, OFF_G1))
        g1_ref[sub * rows:(sub + 1) * rows] = _sigmoid(proj(h, OFF_G1, IN_WIDTH)).astype(BF16)
        pos = pos_ref[:, sub * rows:(sub + 1) * rows].astype(F32)
        ang = jnp.concatenate([invf_ref[...] * pos[:, b * LANES:(b + 1) * LANES]
                               for b in range(rows // LANES)], axis=1)
        n_rot = 2 * HEADS_PER_GROUP
        cos = jnp.concatenate([jnp.cos(ang)] * n_rot + [jnp.ones((LANES - 8 * n_rot, rows), F32)], axis=0).T
        sin = jnp.concatenate([jnp.sin(ang)] * n_rot + [jnp.zeros((LANES - 8 * n_rot, rows), F32)], axis=0).T
        lane = lax.broadcasted_iota(jnp.int32, (rows, LANES), 1)
        sin_a = jnp.where(lane < 32, -sin, 0.0)
        sin_b = jnp.where(lane < 32, 0.0, sin)
        for which, (off, slot0) in enumerate(((OFF_Q, 0), (OFF_K, 2))):
            for g in range(len(DILATIONS)):
                lo = off - OFF_Q + g * GROUP_WIDTH
                t = jnp.dot(h[0], wqk_ref[:, lo:lo + GROUP_WIDTH], preferred_element_type=F32) * h[1]
                t0 = t[:, :LANES]
                t0 = t0 * cos + pltpu.roll(t0, LANES - 32, 1) * sin_a + pltpu.roll(t0, 32, 1) * sin_b
                store_by_residue(sub, (t0, t[:, LANES:]), which, g, slot0 + g - 1)
        for g in range(len(DILATIONS)):
            t = proj(h, OFF_V + g * GROUP_WIDTH, OFF_V + (g + 1) * GROUP_WIDTH)
            store_by_residue(sub, (t[:, :LANES], t[:, LANES:]), 2, g, 4 + g - 1)
        return g0

    def spatial_gating(sub, u, vn, g0):
        bias = bias_ref[...]
        ya_rows = []
        for c in range(rows // GMLP_CHUNK):
            vc = vn[c * GMLP_CHUNK:(c + 1) * GMLP_CHUNK]
            p0 = jnp.dot(wm[0], vc[:, 0:256], preferred_element_type=F32)
            p1 = jnp.dot(wm[1], vc[:, 128:384], preferred_element_type=F32)
            p2 = jnp.dot(wm[2], vc[:, 384:640], preferred_element_type=F32)
            p3 = jnp.dot(wm[3], vc[:, 512:768], preferred_element_type=F32)
            mixed = jnp.concatenate(
                [p0[:, :LANES], jnp.where(low_half, p0[:, LANES:], p1[:, :LANES]), p1[:, LANES:],
                 p2[:, :LANES], jnp.where(low_half, p2[:, LANES:], p3[:, :LANES]), p3[:, LANES:]],
                axis=1) + bias
            ya_rows.append(u[c * GMLP_CHUNK:(c + 1) * GMLP_CHUNK] * mixed)
        ya = jnp.concatenate(ya_rows, axis=0).astype(BF16)
        ta_ref[sub * rows:(sub + 1) * rows] = (
            g0 * jnp.dot(ya, pa_ref[...], preferred_element_type=F32)).astype(BF16)

    chains = [gating_inputs(sub) for sub in subs]
    gates = [gates_and_qkv(sub, chains[sub][0]) for sub in subs]
    for sub in subs:
        spatial_gating(sub, chains[sub][1], chains[sub][2], gates[sub])


def _resident(shape):
    nd = len(shape)
    return pl.BlockSpec(shape, lambda *_: (0,) * nd, pipeline_mode=pl.Buffered(1))


def _rotary_first_permutation():
    perm = np.zeros((GROUP_WIDTH, GROUP_WIDTH), np.float32)
    for h in range(HEADS_PER_GROUP):
        for dim in range(HEAD_DIM):
            if dim < ROT_HALF:
                dst = ROT_HALF * h + dim
            elif dim < ROT_DIM:
                dst = HEADS_PER_GROUP * ROT_HALF + ROT_HALF * h + (dim - ROT_HALF)
            else:
                dst = HEADS_PER_GROUP * ROT_DIM + PASS_DIM * h + (dim - ROT_DIM)
            perm[h * HEAD_DIM + dim, dst] = 1.0
    return jnp.asarray(perm, BF16)


def _projection(x2, pos2, mix_g, w_in, gmlp_g, w_s, bias_tbl, w_a, inv_freq):
    n = x2.shape[0]
    tm = PROJ_TILE
    row_spec = lambda w: pl.BlockSpec((tm, w), lambda i: (i, 0))
    grp_specs = [pl.BlockSpec((tm // d, d * QKV_WIDTH), lambda i: (i, 0)) for d in DILATIONS]
    grp_shapes = [jax.ShapeDtypeStruct((n // d, d * QKV_WIDTH), BF16) for d in DILATIONS]
    wide = jax.ShapeDtypeStruct((n, D_MODEL), BF16)
    return pl.pallas_call(
        _proj_kernel,
        grid=(n // tm,),
        in_specs=[row_spec(D_MODEL), pl.BlockSpec((None, 1, tm), lambda i: (i, 0, 0)),
                  _resident((1, D_MODEL)),
                  _resident((D_MODEL, IN_WIDTH)), _resident((1, GMLP_WIDTH)),
                  _resident((GMLP_GROUPS, GMLP_CHUNK, GMLP_CHUNK)),
                  _resident((GMLP_CHUNK, GMLP_WIDTH)), _resident((GMLP_WIDTH, D_MODEL)),
                  _resident((ROT_HALF, LANES)), _resident((GROUP_WIDTH, GROUP_WIDTH))],
        out_specs=grp_specs + [row_spec(D_MODEL)] * 2,
        out_shape=grp_shapes + [wide] * 2,
        scratch_shapes=[pltpu.VMEM((STAGE_SETS, 6, 2, PROJ_SUBTILE, LANES), F32),
                        pltpu.VMEM((D_MODEL, 2 * len(DILATIONS) * GROUP_WIDTH), BF16),
                        pltpu.VMEM((STAGE_SETS, 3, 2, PROJ_SUBTILE, LANES), F32)],
        compiler_params=pltpu.CompilerParams(
            dimension_semantics=("arbitrary",), vmem_limit_bytes=62 * 1024 * 1024),
        name="projection",
    )(x2, pos2, mix_g, w_in, gmlp_g, w_s, bias_tbl, w_a, inv_freq, _rotary_first_permutation())


def _head_masks():
    lane = lax.broadcasted_iota(jnp.int32, (1, GROUP_WIDTH), 1)
    qk_head = jnp.where(lane < 2 * 32, (lane % 32) // ROT_HALF, (lane - 2 * 32) // PASS_DIM)
    v_head = lane // HEAD_DIM
    return qk_head, v_head


def _attn_kernel(qkv_ref, *rest, n_res, n_blocks, group, n_casts):
    o_ref, lse_ref = rest[n_casts:n_casts + 2]
    for src, dst in zip(rest[:n_casts], rest[n_casts + 2:]):
        dst[...] = src[...].astype(BF16)
    nh = HEADS_PER_GROUP
    blk = ATTN_BLOCK
    qk_head, v_head = _head_masks()
    qi = lax.broadcasted_iota(jnp.int32, (blk, 2 * blk), 0)
    ki = lax.broadcasted_iota(jnp.int32, (blk, 2 * blk), 1)
    band2 = jnp.where((ki >= qi) & (ki <= qi + BAND), 0.0, -jnp.inf).astype(F32)
    band1 = band2[:, blk:]
    band2 = jnp.concatenate([band2] * nh, axis=0)
    band1 = jnp.concatenate([band1] * nh, axis=0)
    low_half = lax.broadcasted_iota(jnp.int32, (blk, LANES), 1) < HEAD_DIM

    def per_head(x):
        halves = [jnp.where(low_half, x[2 * c * blk:(2 * c + 1) * blk], x[(2 * c + 1) * blk:(2 * c + 2) * blk])
                  for c in range(nh // 2)]
        return jnp.concatenate(halves, axis=1)

    def part(cols, which):
        lo = cols.start * 3 + which * GROUP_WIDTH
        return slice(lo, lo + GROUP_WIDTH)

    def scores(cols, q_rows, kv_rows, band):
        qb = qkv_ref[q_rows, part(cols, 0)]
        qs = jnp.concatenate([jnp.where(qk_head == h, qb, jnp.zeros_like(qb)) for h in range(nh)], axis=0)
        return lax.dot_general(qs, qkv_ref[kv_rows, part(cols, 1)], (((1,), (1,)), ((), ())),
                               preferred_element_type=F32) + band

    def softmax(s):
        m = jnp.max(s, axis=-1, keepdims=True)
        p = jnp.exp2(s - m)
        return p.astype(BF16), m, jnp.sum(p, axis=-1, keepdims=True)

    def finish(cols, q_rows, kv_rows, p, m, den):
        pv = jnp.dot(p, qkv_ref[kv_rows, part(cols, 2)], preferred_element_type=F32)
        out = pv[(nh - 1) * blk:]
        for h in range(nh - 2, -1, -1):
            out = jnp.where(v_head == h, pv[h * blk:(h + 1) * blk], out)
        den = per_head(jnp.broadcast_to(den, (nh * blk, LANES)))
        o_ref[q_rows, cols] = (out / den).astype(o_ref.dtype)
        m_wide = per_head(jnp.broadcast_to(m, (nh * blk, LANES)))
        lse_ref[q_rows, cols] = (m_wide + jnp.log2(den)) * np.float32(np.log(2.0))

    def block_group(cols, blocks):
        s = [scores(cols, *b) for b in blocks]
        pmd = [softmax(x) for x in s]
        for b, (p, m, den) in zip(blocks, pmd):
            finish(cols, b[0], b[1], p, m, den)

    def later_block(q0):
        return slice(q0, q0 + blk), slice(q0 - blk, q0 + blk), band2

    for r in range(n_res):
        cols = slice(r * GROUP_WIDTH, (r + 1) * GROUP_WIDTH)
        first = [(slice(0, blk), slice(0, blk), band1)]
        block_group(cols, first + [later_block(j * blk) for j in range(1, group)])
        for i in range(1, n_blocks // group):
            block_group(cols, [later_block(j * blk) for j in range(i * group, (i + 1) * group)])


def _attention(qkv, batch, seq, dil, res_per_step, casts):
    sub = seq // dil
    width = res_per_step * GROUP_WIDTH
    spec = pl.BlockSpec((None, sub, width), lambda b, r: (b, 0, r))
    qkv_spec = pl.BlockSpec((None, sub, 3 * width), lambda b, r: (b, 0, r))
    n_blocks = sub // ATTN_BLOCK
    cast_specs = [pl.BlockSpec((w.shape[0] // batch, w.shape[1]), lambda b, r: (b, 0)) for w in casts]
    kern = functools.partial(_attn_kernel, n_res=res_per_step, n_blocks=n_blocks,
                             group=min(ATTN_GROUP, n_blocks), n_casts=len(casts))
    o, lse, *cast = pl.pallas_call(
        kern,
        grid=(batch, dil // res_per_step),
        in_specs=[qkv_spec] + cast_specs,
        out_specs=[spec, spec] + cast_specs,
        out_shape=[jax.ShapeDtypeStruct((batch, sub, dil * GROUP_WIDTH), BF16),
                   jax.ShapeDtypeStruct((batch, sub, dil * GROUP_WIDTH), F32)]
                  + [jax.ShapeDtypeStruct(w.shape, BF16) for w in casts],
        compiler_params=pltpu.CompilerParams(
            dimension_semantics=("arbitrary", "arbitrary"), vmem_limit_bytes=VMEM_LIMIT_BYTES),
        name=f"attention_d{dil}",
    )(qkv.reshape(batch, sub, dil * QKV_WIDTH), *casts)
    return o.reshape(batch * sub, dil * GROUP_WIDTH), lse.reshape(batch * sub, dil * GROUP_WIDTH), cast


def _ffn_kernel(x_ref, ta_ref, g1_ref, o0_ref, o1_ref, o2_ref, l0_ref, l1_ref, l2_ref,
                pb_ref, wo_ref, fg_ref, wup_ref, cw_ref, cb_ref, wd_ref, og_ref,
                out_ref, carry_ref, act_ref, stage_ref, mid_ref):
    tm = x_ref.shape[0]
    rows = FFN_SUBTILE
    subs = range(tm // rows)

    @pl.when(pl.program_id(1) == 0)
    def _():
        carry_ref[...] = jnp.zeros_like(carry_ref)

    def natural(sub, ref, g, slot):
        dil = DILATIONS[g]
        view_rows = slice(sub * rows // dil, (sub + 1) * rows // dil)
        if dil == 1:
            return ref[view_rows, :].astype(F32)
        for r in range(dil):
            for s in range(2):
                piece = ref[view_rows, r * GROUP_WIDTH + s * LANES:r * GROUP_WIDTH + (s + 1) * LANES].astype(F32)
                if dil == 4:
                    stage_ref[sub, slot, s, pl.ds(r, rows // 4, stride=4), :] = piece
                else:
                    r1, r2 = r % 4, r // 4
                    mid_ref[sub, slot // 2, s, pl.ds(r1 * (rows // 4) + r2, rows // 16, stride=4), :] = piece
        if dil == 16:
            for r1 in range(4):
                for s in range(2):
                    stage_ref[sub, slot, s, pl.ds(r1, rows // 4, stride=4), :] = (
                        mid_ref[sub, slot // 2, s, r1 * (rows // 4):(r1 + 1) * (rows // 4), :])
        return jnp.concatenate([stage_ref[sub, slot, 0], stage_ref[sub, slot, 1]], axis=1)

    def merged_branches(sub):
        tok = slice(sub * rows, (sub + 1) * rows)
        l0, l1, l2 = natural(sub, l0_ref, 0, 0), natural(sub, l1_ref, 1, 0), natural(sub, l2_ref, 2, 1)
        lmax = jnp.maximum(jnp.maximum(l0, l1), l2)
        e0, e1, e2 = jnp.exp(l0 - lmax), jnp.exp(l1 - lmax), jnp.exp(l2 - lmax)
        yb = (e0 * natural(sub, o0_ref, 0, 0) + e1 * natural(sub, o1_ref, 1, 2)
              + e2 * natural(sub, o2_ref, 2, 3)) / (e0 + e1 + e2)
        merged = ta_ref[tok, :].astype(F32) + g1_ref[tok, :].astype(F32) * jnp.dot(
            yb.astype(BF16), pb_ref[...], preferred_element_type=F32)
        return merged.astype(BF16)

    def out_proj(sub, merged):
        tok = slice(sub * rows, (sub + 1) * rows)
        x1 = x_ref[tok, :] + jnp.dot(merged, wo_ref[...], preferred_element_type=F32)
        return x1, _rms_deferred(x1, fg_ref[...])

    row8 = lax.broadcasted_iota(jnp.int32, (8, FF_CHUNK), 0)

    def up_chunk(sub, c, h2, prev):
        cols = slice(c * FF_CHUNK, (c + 1) * FF_CHUNK)
        xg, scale = h2
        a = jnp.dot(xg, wup_ref[:, cols], preferred_element_type=F32) * scale
        val = jnp.dot(xg, wup_ref[:, D_FF + c * FF_CHUNK:D_FF + (c + 1) * FF_CHUNK],
                      preferred_element_type=F32) * scale
        s1 = pltpu.roll(a, 1, 0)
        s2 = pltpu.roll(a, 2, 0)
        s1 = jnp.concatenate([jnp.where(row8 < 1, pltpu.roll(prev, 1, 0), s1[:8]), s1[8:]], axis=0)
        s2 = jnp.concatenate([jnp.where(row8 < 2, pltpu.roll(prev, 2, 0), s2[:8]), s2[8:]], axis=0)
        w = cw_ref[:, cols]
        y = s2 * w[0:1] + s1 * w[1:2] + a * w[2:3] + cb_ref[:, cols]
        act_ref[sub * rows:(sub + 1) * rows, cols] = (_gelu(y) * val).astype(BF16)
        return a[rows - 8:]

    xh = [out_proj(sub, merged_branches(sub)) for sub in subs]
    for c in range(N_FF_CHUNKS):
        cols = slice(c * FF_CHUNK, (c + 1) * FF_CHUNK)
        prev = carry_ref[:, cols]
        for sub in subs:
            prev = up_chunk(sub, c, xh[sub][1], prev)
        carry_ref[:, cols] = prev
    for sub in subs:
        tok = slice(sub * rows, (sub + 1) * rows)
        x2 = xh[sub][0] + jnp.dot(act_ref[tok, :], wd_ref[...], preferred_element_type=F32)
        out_ref[tok, :] = _rms(x2, og_ref[...])


def _ffn(x, ta, g1, os_, ls_, w_b, w_out, ffn_g, w_up, conv_w, conv_b, w_down, final_g, batch, seq):
    tm = FFN_TILE
    tiles = seq // tm
    row_spec = lambda w: pl.BlockSpec((tm, w), lambda b, i: (b * tiles + i, 0))
    grp_specs = [pl.BlockSpec((tm // d, d * GROUP_WIDTH), lambda b, i: (b * tiles + i, 0))
                 for d in DILATIONS]
    return pl.pallas_call(
        _ffn_kernel,
        grid=(batch, tiles),
        in_specs=[row_spec(D_MODEL), row_spec(D_MODEL), row_spec(D_MODEL)]
                 + grp_specs * 2
                 + [_resident((GROUP_WIDTH, D_MODEL)), _resident((D_MODEL, D_MODEL)),
                    _resident((1, D_MODEL)), _resident((D_MODEL, 2 * D_FF)),
                    _resident((3, D_FF)), _resident((1, D_FF)), _resident((D_FF, D_MODEL)),
                    _resident((1, D_MODEL))],
        out_specs=row_spec(D_MODEL),
        out_shape=jax.ShapeDtypeStruct((batch * seq, D_MODEL), F32),
        scratch_shapes=[pltpu.VMEM((8, D_FF), F32), pltpu.VMEM((tm, D_FF), BF16),
                        pltpu.VMEM((tm // FFN_SUBTILE, 4, 2, FFN_SUBTILE, LANES), F32),
                        pltpu.VMEM((tm // FFN_SUBTILE, 2, 2, FFN_SUBTILE, LANES), F32)],
        compiler_params=pltpu.CompilerParams(
            dimension_semantics=("arbitrary", "arbitrary"), vmem_limit_bytes=VMEM_LIMIT_BYTES),
        name="mix_ffn",
    )(x, ta, g1, *os_, *ls_, w_b, w_out, ffn_g, w_up, conv_w, conv_b, w_down, final_g)


def kernel(x, positions, mix_norm_g, w_in, gmlp_norm_g, w_spatial, b_spatial, w_branch_a,
           w_branch_b, w_out, ffn_norm_g, w_up, conv_w, conv_b, w_down, final_norm_g):
    batch, seq, d = x.shape
    assert w_in.shape[0] == 1, "single-layer block: the final norm is fused into the FFN stage"
    layer = 0
    n = batch * seq
    inv_freq = ROPE_THETA ** (-jnp.arange(0, ROT_DIM, 2, dtype=F32) / ROT_DIM)
    inv_freq = jnp.broadcast_to(inv_freq[:, None], (ROT_HALF, LANES))
    pos2 = positions.reshape(n // PROJ_TILE, 1, PROJ_TILE)
    xf = x.reshape(n, d)
    col_scale = np.ones((IN_WIDTH,), np.float32)
    col_scale[OFF_Q:OFF_K] = HEAD_DIM ** -0.5
    col_scale[OFF_K:OFF_V] = np.log2(np.e)
    wi = (w_in[layer] * col_scale).astype(BF16)
    bias_tbl = jnp.repeat(b_spatial[layer].T, GMLP_WIDTH // GMLP_GROUPS, axis=1)
    res = _projection(xf, pos2, mix_norm_g[layer].reshape(1, d), wi,
                      gmlp_norm_g[layer].reshape(1, GMLP_WIDTH), w_spatial[layer], bias_tbl,
                      w_branch_a[layer].astype(BF16), inv_freq)
    qkvs, ta, g1 = res[0:3], res[3], res[4]
    side_casts = ([w_up[layer]], [w_down[layer]], [w_out[layer], w_branch_b[layer]])
    outs, lses, cast = [], [], []
    for g, (dil, rps) in enumerate(zip(DILATIONS, RESIDUES_PER_STEP)):
        o, lse, c = _attention(qkvs[g], batch, seq, dil, rps, side_casts[g])
        outs.append(o)
        lses.append(lse)
        cast += c
    w_up_b, w_down_b, w_out_b, w_b_b = cast
    out = _ffn(xf, ta, g1, outs, lses, w_b_b, w_out_b,
               ffn_norm_g[layer].reshape(1, d), w_up_b, conv_w[layer],
               conv_b[layer].reshape(1, D_FF), w_down_b,
               final_norm_g.reshape(1, d), batch, seq)
    return out.reshape(batch, seq, d)
```
